```python
import jax
import jax.numpy as jnp
from jax import lax
import numpy as np

D_MODEL = 1024
BATCH = 8
SEQ = 8192
DEPTH = 4

GRID_W = 64
CTX_LEN = 256
EPS = 1e-6

NA_HEAD_DIM = 64
NA_HEADS = D_MODEL // (2 * NA_HEAD_DIM)
NA_WIN_ROWS = 8
NA_WIN_COLS = 16
NA_W = NA_HEADS * NA_HEAD_DIM

GDN_HEAD_DIM = 128
GDN_HEADS = D_MODEL // (2 * GDN_HEAD_DIM)
GDN_W = GDN_HEADS * GDN_HEAD_DIM
GDN_CONV = 5
GDN_CHUNK = 64

EVEN_IN = 3 * NA_W + 4 * GDN_W + 4 * GDN_HEADS
EVEN_MIX = NA_W + GDN_W

MLA_V = 128
MLA_HEADS = D_MODEL // MLA_V
MLA_NOPE = 128
MLA_ROPE = 64
MLA_Q_RANK = 384
MLA_KV_RANK = 256
MLA_IN = MLA_Q_RANK + MLA_KV_RANK + MLA_ROPE
MLA_QBLOCK = 128
ROPE_THETA = 10000.0

PEER_KEYS = 128
PEER_EXPERTS = PEER_KEYS * PEER_KEYS
PEER_HEADS = 8
PEER_TOPK = 16
PEER_DKEY = 256
PEER_BLOCK = 128

N_EVEN = (DEPTH + 1) // 2
N_ODD = DEPTH // 2

kernel_name = 'hybrid_na_gdn_mla_peer_diffusion_trunk'


def rmsnorm(x, g):
    x32 = x.astype(jnp.float32)
    y = x32 * lax.rsqrt(jnp.mean(x32 * x32, axis=-1, keepdims=True) + EPS)
    return (y * g.astype(jnp.float32)).astype(x.dtype)


def l2norm(x):
    x32 = x.astype(jnp.float32)
    return x32 * lax.rsqrt(jnp.sum(x32 * x32, axis=-1, keepdims=True) + EPS)


def modulate(h, shift, scale):
    return h * (1.0 + scale[:, None, :]) + shift[:, None, :]


def split_heads(t, n_heads):
    b, l, w = t.shape
    return t.reshape(b, l, n_heads, w // n_heads).transpose(0, 2, 1, 3)


def merge_heads(t):
    b, h, l, d = t.shape
    return t.transpose(0, 2, 1, 3).reshape(b, l, h * d)


def axial_rope_tables(n_tokens, dim):
    half = dim // 2
    inv_freq = ROPE_THETA ** (-jnp.arange(0, half, 2, dtype=jnp.float32) / half)
    t = jnp.arange(n_tokens, dtype=jnp.int32)
    row = (t // GRID_W).astype(jnp.float32)
    col = (t % GRID_W).astype(jnp.float32)
    ang_r = row[:, None] * inv_freq[None, :]
    ang_c = col[:, None] * inv_freq[None, :]
    ang = jnp.concatenate([ang_r, ang_r, ang_c, ang_c], axis=-1)
    return jnp.cos(ang), jnp.sin(ang)


def apply_axial_rope(x, cos, sin):
    a, b, cq, d = jnp.split(x, 4, axis=-1)
    rot = jnp.concatenate([-b, a, -d, cq], axis=-1)
    return x * cos.astype(x.dtype) + rot * sin.astype(x.dtype)


def softmax_attend(q, k, v, scale):
    s = jnp.einsum('bhqd,bhkd->bhqk', q, k).astype(jnp.float32) * scale
    p = jax.nn.softmax(s, axis=-1).astype(v.dtype)
    return jnp.einsum('bhqk,bhkd->bhqd', p, v)


def blocked_attention(q, k, v, scale):
    b, h, l, dq = q.shape
    qb = jnp.moveaxis(q.reshape(b, h, l // MLA_QBLOCK, MLA_QBLOCK, dq), 2, 0)
    o = lax.map(lambda qi: softmax_attend(qi, k, v, scale), qb)
    return jnp.moveaxis(o, 0, 2).reshape(b, h, l, v.shape[-1])


def neighbourhood_attention(q, k, v, k_ctx, v_ctx, rpb):
    b, h, l, dh = q.shape
    rows = l // GRID_W
    wr = min(NA_WIN_ROWS, rows)
    wc = NA_WIN_COLS
    n_ctx = k_ctx.shape[2]
    scale = dh ** -0.5
    k_grid = k.reshape(b, h, rows, GRID_W, dh)
    v_grid = v.reshape(b, h, rows, GRID_W, dh)
    q_rows = jnp.moveaxis(q.reshape(b, h, rows, GRID_W, dh), 2, 0)
    q_col = jnp.arange(GRID_W)
    c_start = jnp.clip(q_col - wc // 2, 0, GRID_W - wc)
    k_col = jnp.tile(jnp.arange(GRID_W), wr)
    k_row_off = jnp.repeat(jnp.arange(wr), GRID_W)
    col_mask = (k_col[None, :] >= c_start[:, None]) & (k_col[None, :] < c_start[:, None] + wc)
    dc_idx = jnp.clip(k_col[None, :] - q_col[:, None], 1 - wc, wc - 1) + (NA_WIN_COLS - 1)

    def one_row(args):
        q_r, r = args
        r_start = jnp.clip(r - wr // 2, 0, rows - wr)
        k_r = lax.dynamic_slice_in_dim(k_grid, r_start, wr, axis=2).reshape(b, h, wr * GRID_W, dh)
        v_r = lax.dynamic_slice_in_dim(v_grid, r_start, wr, axis=2).reshape(b, h, wr * GRID_W, dh)
        dr_idx = r_start + k_row_off - r + (NA_WIN_ROWS - 1)
        bias = rpb[:, dr_idx[None, :], dc_idx].astype(jnp.float32)
        s_lat = jnp.einsum('bhqd,bhkd->bhqk', q_r, k_r).astype(jnp.float32) * scale + bias
        s_lat = jnp.where(col_mask, s_lat, -jnp.inf)
        s_ctx = jnp.einsum('bhqd,bhkd->bhqk', q_r, k_ctx).astype(jnp.float32) * scale
        p = jax.nn.softmax(jnp.concatenate([s_ctx, s_lat], axis=-1), axis=-1).astype(v.dtype)
        return (jnp.einsum('bhqk,bhkd->bhqd', p[..., :n_ctx], v_ctx)
                + jnp.einsum('bhqk,bhkd->bhqd', p[..., n_ctx:], v_r))

    out = lax.map(one_row, (q_rows, jnp.arange(rows)))
    return jnp.moveaxis(out, 0, 2).reshape(b, h, l, dh)


def short_conv(x, w):
    ch = x.shape[-1]
    return lax.conv_general_dilated(
        x, w.astype(x.dtype)[:, None, :], window_strides=(1,),
        padding=[(GDN_CONV // 2, GDN_CONV // 2)],
        dimension_numbers=('NWC', 'WIO', 'NWC'), feature_group_count=ch)


def gdn_chunk_scan(q, k, v, g, beta, s0, want_out):
    b, h, l, dk = q.shape
    dv = v.shape[-1]
    n = l // GDN_CHUNK
    f32 = jnp.float32
    q = q.astype(f32).reshape(b, h, n, GDN_CHUNK, dk)
    k = k.astype(f32).reshape(b, h, n, GDN_CHUNK, dk)
    v = v.astype(f32).reshape(b, h, n, GDN_CHUNK, dv)
    g = g.astype(f32).reshape(b, h, n, GDN_CHUNK)
    beta = beta.astype(f32).reshape(b, h, n, GDN_CHUNK)
    gc = jnp.cumsum(g, axis=-1)
    idx = jnp.arange(GDN_CHUNK)
    incl = idx[:, None] >= idx[None, :]
    strict = idx[:, None] > idx[None, :]
    dec = jnp.exp(jnp.where(incl, gc[..., :, None] - gc[..., None, :], -jnp.inf))
    a_mat = jnp.where(strict, beta[..., :, None] * jnp.einsum('bhncd,bhnkd->bhnck', k, k) * dec, 0.0)
    w_v = lax.linalg.triangular_solve(a_mat, beta[..., None] * v, left_side=True, lower=True,
                                      unit_diagonal=True)
    w_k = lax.linalg.triangular_solve(a_mat, (beta * jnp.exp(gc))[..., None] * k, left_side=True,
                                      lower=True, unit_diagonal=True)
    k_end = k * jnp.exp(gc[..., -1:] - gc)[..., None]
    b_end = jnp.exp(gc[..., -1])
    mv = lambda t: jnp.moveaxis(t, 2, 0)
    if want_out:
        p_in = jnp.einsum('bhncd,bhnkd->bhnck', q, k) * dec
        q_dec = q * jnp.exp(gc)[..., None]

        def step(s, xs):
            wv, wk, ke, be, qd, pp = xs
            u = wv - jnp.einsum('bhcd,bhde->bhce', wk, s)
            o = jnp.einsum('bhcd,bhde->bhce', qd, s) + jnp.einsum('bhck,bhke->bhce', pp, u)
            s = be[..., None, None] * s + jnp.einsum('bhcd,bhce->bhde', ke, u)
            return s, o

        s_fin, o = lax.scan(step, s0, (mv(w_v), mv(w_k), mv(k_end), mv(b_end), mv(q_dec), mv(p_in)))
        return s_fin, jnp.moveaxis(o, 0, 2).reshape(b, h, l, dv)

    def step_state(s, xs):
        wv, wk, ke, be = xs
        u = wv - jnp.einsum('bhcd,bhde->bhce', wk, s)
        return be[..., None, None] * s + jnp.einsum('bhcd,bhce->bhde', ke, u), None

    s_fin, _ = lax.scan(step_state, s0, (mv(w_v), mv(w_k), mv(k_end), mv(b_end)))
    return s_fin, None


def gdn_inputs(p, conv_w, a_log, dt_bias):
    b, l, _ = p.shape
    qkv = jax.nn.silu(short_conv(p[..., :3 * GDN_W], conv_w))
    q = l2norm(split_heads(qkv[..., :GDN_W], GDN_HEADS)) * (GDN_HEAD_DIM ** -0.5)
    k = l2norm(split_heads(qkv[..., GDN_W:2 * GDN_W], GDN_HEADS))
    v = split_heads(qkv[..., 2 * GDN_W:], GDN_HEADS)
    z = p[..., 3 * GDN_W:4 * GDN_W]
    gr = p[..., 4 * GDN_W:].astype(jnp.float32).reshape(b, l, 2, 2, GDN_HEADS).transpose(2, 3, 0, 4, 1)
    g = -jnp.exp(a_log.astype(jnp.float32))[:, None, :, None] * jax.nn.softplus(
        gr[:, 0] + dt_bias.astype(jnp.float32)[:, None, :, None])
    beta = jax.nn.sigmoid(gr[:, 1])
    return q, k, v, z, g, beta


def gdn_output(o, z, norm_g):
    b, h, l, dv = o.shape
    o = rmsnorm(o.transpose(0, 2, 1, 3), norm_g)
    gate = jax.nn.silu(z.astype(jnp.float32)).reshape(b, l, h, dv)
    return (o * gate).reshape(b, l, h * dv).astype(z.dtype)


def gated_deltanet(p, pc, conv_w, a_log, dt_bias, norm_g, want_ctx):
    q, k, v, z, g, beta = gdn_inputs(p, conv_w, a_log, dt_bias)
    qc, kc, vc, zc, gcx, betac = gdn_inputs(pc, conv_w, a_log, dt_bias)
    flip = lambda t: jnp.flip(t, axis=2)
    zero = jnp.zeros((q.shape[0], GDN_HEADS, GDN_HEAD_DIM, GDN_HEAD_DIM), jnp.float32)
    s_f, oc_f = gdn_chunk_scan(qc, kc, vc, gcx[0], betac[0], zero, want_ctx)
    s_b, oc_b = gdn_chunk_scan(flip(qc), flip(kc), flip(vc), flip(gcx[1]), flip(betac[1]), zero, want_ctx)
    _, o_f = gdn_chunk_scan(q, k, v, g[0], beta[0], s_f, True)
    _, o_b = gdn_chunk_scan(flip(q), flip(k), flip(v), flip(g[1]), flip(beta[1]), s_b, True)
    y = gdn_output(o_f + flip(o_b), z, norm_g)
    if not want_ctx:
        return y, None
    return y, gdn_output(oc_f + flip(oc_b), zc, norm_g)


def even_mixer(h, hc, w_in, conv_w, a_log, dt_bias, gdn_g, rpb, w_out, want_ctx):
    p = h @ w_in
    pc = hc @ w_in
    q = split_heads(p[..., :NA_W], NA_HEADS)
    k = split_heads(p[..., NA_W:2 * NA_W], NA_HEADS)
    v = split_heads(p[..., 2 * NA_W:3 * NA_W], NA_HEADS)
    kc = split_heads(pc[..., NA_W:2 * NA_W], NA_HEADS)
    vc = split_heads(pc[..., 2 * NA_W:3 * NA_W], NA_HEADS)
    y_a = merge_heads(neighbourhood_attention(q, k, v, kc, vc, rpb))
    y_b, yc_b = gated_deltanet(p[..., 3 * NA_W:], pc[..., 3 * NA_W:], conv_w, a_log, dt_bias, gdn_g, want_ctx)
    y = jnp.concatenate([y_a, y_b], axis=-1) @ w_out
    if not want_ctx:
        return y, None
    qc = split_heads(pc[..., :NA_W], NA_HEADS)
    yc_a = merge_heads(softmax_attend(qc, kc, vc, NA_HEAD_DIM ** -0.5))
    return y, jnp.concatenate([yc_a, yc_b], axis=-1) @ w_out


def mla_q(p, q_g, w_uq, rope):
    cq = rmsnorm(p[..., :MLA_Q_RANK], q_g)
    q = split_heads(cq @ w_uq, MLA_HEADS)
    q_nope, q_rope = q[..., :MLA_NOPE], q[..., MLA_NOPE:]
    if rope is not None:
        q_rope = apply_axial_rope(q_rope, rope[0], rope[1])
    return jnp.concatenate([q_nope, q_rope], axis=-1)


def mla_kv(p, kv_g, w_ukv, rope):
    b, l, _ = p.shape
    ckv = rmsnorm(p[..., MLA_Q_RANK:MLA_Q_RANK + MLA_KV_RANK], kv_g)
    kv = split_heads(ckv @ w_ukv, MLA_HEADS)
    k_nope, v = kv[..., :MLA_NOPE], kv[..., MLA_NOPE:]
    k_rope = p[..., MLA_Q_RANK + MLA_KV_RANK:][:, None]
    if rope is not None:
        k_rope = apply_axial_rope(k_rope, rope[0], rope[1])
    k = jnp.concatenate([k_nope, jnp.broadcast_to(k_rope, (b, MLA_HEADS, l, MLA_ROPE))], axis=-1)
    return k, v


def odd_mixer(h, hc, w_in, q_g, kv_g, w_uq, w_ukv, w_out, rope, want_ctx):
    p = h @ w_in
    pc = hc @ w_in
    scale = (MLA_NOPE + MLA_ROPE) ** -0.5
    q = mla_q(p, q_g, w_uq, rope)
    k, v = mla_kv(p, kv_g, w_ukv, rope)
    kc, vc = mla_kv(pc, kv_g, w_ukv, None)
    o = blocked_attention(q, jnp.concatenate([kc, k], axis=2), jnp.concatenate([vc, v], axis=2), scale)
    y = merge_heads(o) @ w_out
    if not want_ctx:
        return y, None
    qc = mla_q(pc, q_g, w_uq, None)
    return y, merge_heads(softmax_attend(qc, kc, vc, scale)) @ w_out


def peer_ffn(h, w_q, sub_keys, u_tab, v_tab):
    b, l, d = h.shape
    kk = PEER_TOPK
    blocks = h.reshape(b * l // PEER_BLOCK, PEER_BLOCK, d)

    def one_block(xb):
        t = xb.shape[0]
        qy = (xb @ w_q).reshape(t, PEER_HEADS, 2, PEER_DKEY // 2)
        s = jnp.einsum('thpd,hpnd->thpn', qy, sub_keys).astype(jnp.float32)
        s_top, i_top = lax.top_k(s, kk)
        comb = (s_top[:, :, 0, :, None] + s_top[:, :, 1, None, :]).reshape(t, PEER_HEADS, kk * kk)
        c_top, c_idx = lax.top_k(comb, kk)
        i1 = jnp.take_along_axis(i_top[:, :, 0], c_idx // kk, axis=-1)
        i2 = jnp.take_along_axis(i_top[:, :, 1], c_idx % kk, axis=-1)
        expert = i1 * PEER_KEYS + i2
        gate = jax.nn.softmax(c_top, axis=-1)
        u = u_tab[expert]
        vv = v_tab[expert]
        act = jax.nn.gelu(jnp.einsum('td,thkd->thk', xb, u).astype(jnp.float32), approximate=False)
        return jnp.einsum('thk,thkd->td', (gate * act).astype(xb.dtype), vv)

    return lax.map(one_block, blocks).reshape(b, l, d)


def setup_inputs(seed: int = 0) -> dict:
    key = jax.random.key(seed)
    ks = iter(jax.random.split(key, 32))
    f32 = jnp.float32
    nrm = lambda shape, s: jax.random.normal(next(ks), shape, f32) * s
    gain = lambda shape: 1.0 + 0.02 * jax.random.normal(next(ks), shape, f32)
    dt = jnp.exp(jax.random.uniform(next(ks), (N_EVEN, 2, GDN_HEADS), f32, minval=-6.9, maxval=-2.3))
    return {
        'x': nrm((BATCH, SEQ, D_MODEL), 1.0),
        'c': nrm((BATCH, D_MODEL), 1.0),
        'ctx': nrm((BATCH, CTX_LEN, D_MODEL), 1.0),
        'c_ctx': nrm((D_MODEL,), 1.0),
        'ada_w': nrm((DEPTH, D_MODEL, 6 * D_MODEL), 0.5 * D_MODEL ** -0.5),
        'ada_b': nrm((DEPTH, 6 * D_MODEL), 0.01),
        'norm1_g': gain((DEPTH, D_MODEL)),
        'norm2_g': gain((DEPTH, D_MODEL)),
        'final_g': gain((D_MODEL,)),
        'even_w_in': nrm((N_EVEN, D_MODEL, EVEN_IN), D_MODEL ** -0.5),
        'even_conv_w': nrm((N_EVEN, GDN_CONV, 3 * GDN_W), GDN_CONV ** -0.5),
        'gdn_a_log': jnp.log(jax.random.uniform(next(ks), (N_EVEN, 2, GDN_HEADS), f32, minval=1.0, maxval=16.0)),
        'gdn_dt_bias': dt + jnp.log(-jnp.expm1(-dt)),
        'gdn_norm_g': gain((N_EVEN, GDN_HEAD_DIM)),
        'na_rpb': nrm((N_EVEN, NA_HEADS, 2 * NA_WIN_ROWS - 1, 2 * NA_WIN_COLS - 1), 0.1),
        'even_w_out': nrm((N_EVEN, EVEN_MIX, D_MODEL), EVEN_MIX ** -0.5),
        'mla_w_in': nrm((N_ODD, D_MODEL, MLA_IN), D_MODEL ** -0.5),
        'mla_q_g': gain((N_ODD, MLA_Q_RANK)),
        'mla_kv_g': gain((N_ODD, MLA_KV_RANK)),
        'mla_w_uq': nrm((N_ODD, MLA_Q_RANK, MLA_HEADS * (MLA_NOPE + MLA_ROPE)), MLA_Q_RANK ** -0.5),
        'mla_w_ukv': nrm((N_ODD, MLA_KV_RANK, MLA_HEADS * (MLA_NOPE + MLA_V)), MLA_KV_RANK ** -0.5),
        'mla_w_out': nrm((N_ODD, MLA_HEADS * MLA_V, D_MODEL), (MLA_HEADS * MLA_V) ** -0.5),
        'peer_w_q': nrm((DEPTH, D_MODEL, PEER_HEADS * PEER_DKEY), D_MODEL ** -0.5),
        'peer_sub_keys': nrm((DEPTH, PEER_HEADS, 2, PEER_KEYS, PEER_DKEY // 2), (PEER_DKEY // 2) ** -0.5),
        'peer_u': nrm((DEPTH, PEER_EXPERTS, D_MODEL), D_MODEL ** -0.5),
        'peer_v': nrm((DEPTH, PEER_EXPERTS, D_MODEL), PEER_HEADS ** -0.5),
    }


def reference(x, c, ctx, c_ctx, ada_w, ada_b, norm1_g, norm2_g, final_g,
              even_w_in, even_conv_w, gdn_a_log, gdn_dt_bias, gdn_norm_g, na_rpb, even_w_out,
              mla_w_in, mla_q_g, mla_kv_g, mla_w_uq, mla_w_ukv, mla_w_out,
              peer_w_q, peer_sub_keys, peer_u, peer_v):
    rope = axial_rope_tables(x.shape[1], MLA_ROPE)
    cond = jax.nn.silu(c)
    cond_ctx = jax.nn.silu(c_ctx)[None, :]
    for layer in range(DEPTH):
        want_ctx = layer < DEPTH - 1
        i = layer // 2
        sh1, sc1, g1, sh2, sc2, g2 = jnp.split(cond @ ada_w[layer] + ada_b[layer], 6, axis=-1)
        csh1, csc1, cg1, csh2, csc2, cg2 = jnp.split(cond_ctx @ ada_w[layer] + ada_b[layer], 6, axis=-1)
        h = modulate(rmsnorm(x, norm1_g[layer]), sh1, sc1)
        hc = modulate(rmsnorm(ctx, norm1_g[layer]), csh1, csc1)
        if layer % 2 == 0:
            y, yc = even_mixer(h, hc, even_w_in[i], even_conv_w[i], gdn_a_log[i], gdn_dt_bias[i],
                               gdn_norm_g[i], na_rpb[i], even_w_out[i], want_ctx)
        else:
            y, yc = odd_mixer(h, hc, mla_w_in[i], mla_q_g[i], mla_kv_g[i], mla_w_uq[i], mla_w_ukv[i],
                              mla_w_out[i], rope, want_ctx)
        x = x + g1[:, None, :] * y
        x = x + g2[:, None, :] * peer_ffn(modulate(rmsnorm(x, norm2_g[layer]), sh2, sc2),
                                          peer_w_q[layer], peer_sub_keys[layer], peer_u[layer], peer_v[layer])
        if want_ctx:
            ctx = ctx + cg1[:, None, :] * yc
            ctx = ctx + cg2[:, None, :] * peer_ffn(modulate(rmsnorm(ctx, norm2_g[layer]), csh2, csc2),
                                                   peer_w_q[layer], peer_sub_keys[layer], peer_u[layer],
                                                   peer_v[layer])
    return rmsnorm(x, final_g)
```

```python
import functools

import jax
import jax.numpy as jnp
from jax import lax
from jax.experimental import pallas as pl
from jax.experimental.pallas import tpu as pltpu

F32 = jnp.float32
BF16 = jnp.bfloat16
I32 = jnp.int32
HIGHEST = lax.Precision.HIGHEST

EPS = 1e-6
GRID_W = 64
NA_HEAD_DIM = 64
NA_HEADS = 8
NA_WIN_ROWS = 8
NA_WIN_COLS = 16
NA_W = NA_HEADS * NA_HEAD_DIM
GDN_HEAD_DIM = 128
GDN_HEADS = 4
GDN_W = GDN_HEADS * GDN_HEAD_DIM
GDN_CONV = 5
GDN_CHUNK = 64
MLA_HEADS = 8
MLA_NOPE = 128
MLA_ROPE = 64
MLA_V = 128
MLA_Q_RANK = 384
MLA_KV_RANK = 256
MLA_QK_PAD = 256
ROPE_THETA = 10000.0
PEER_KEYS = 128
PEER_HEADS = 8
PEER_TOPK = 16
PEER_SEL = PEER_HEADS * PEER_TOPK

TOK_TILE = 256
LANE = 128
HALO = 8
PEER_GROUP = 8
NEG = -1e30
VMEM_LIMIT = 56 * 1024 * 1024


def _params(*sem):
    return pltpu.CompilerParams(dimension_semantics=sem, vmem_limit_bytes=VMEM_LIMIT)


def _nt(a, b, precision=None):
    return lax.dot_general(a, b, (((1,), (1,)), ((), ())), precision=precision,
                           preferred_element_type=F32)


def _mm(a, b, precision=None):
    return jnp.dot(a, b, precision=precision, preferred_element_type=F32)


def _rms(x):
    return x * lax.rsqrt(jnp.mean(x * x, axis=-1, keepdims=True) + EPS)


def _norm_mod(x, g, sh, sc):
    return (_rms(x) * g) * (1.0 + sc) + sh


def _silu(x):
    return x * jax.nn.sigmoid(x)


def _ada_body(c_ref, w_ref, b_ref, o_ref):
    o_ref[0] = _mm(_silu(c_ref[...]), w_ref[0], HIGHEST) + b_ref[0]


def _ada(cc, ada_w, ada_b):
    depth, d, n = ada_w.shape
    tn = n // 4
    return pl.pallas_call(
        _ada_body, grid=(depth, n // tn),
        in_specs=[pl.BlockSpec(cc.shape, lambda l, j: (0, 0)),
                  pl.BlockSpec((1, d, tn), lambda l, j: (l, 0, j)),
                  pl.BlockSpec((1, 1, tn), lambda l, j: (l, 0, j))],
        out_specs=pl.BlockSpec((1, cc.shape[0], tn), lambda l, j: (l, 0, j)),
        out_shape=jax.ShapeDtypeStruct((depth, cc.shape[0], n), F32),
        compiler_params=_params("arbitrary", "arbitrary"), name="ada_mod",
    )(cc, ada_w, ada_b.reshape(depth, 1, n))


def _tok_spec(width, col=0):
    return pl.BlockSpec((None, TOK_TILE, width), lambda b, t: (b, t, col))


def _mod_spec(d, n_lat_tiles):
    return pl.BlockSpec((None, None, 1, d), lambda b, t: (b, jnp.minimum(t // n_lat_tiles, 1), 0, 0))


def _full_spec(shape):
    nd = len(shape)
    return pl.BlockSpec(shape, lambda b, t: (0,) * nd)


def _even_in_body(x_ref, g_ref, sh_ref, sc_ref, w_ref, na_ref, qkv_ref, z_ref, gate_ref):
    h = _norm_mod(x_ref[...], g_ref[...], sh_ref[...], sc_ref[...]).astype(BF16)
    c0, c1, c2 = 3 * NA_W, 3 * NA_W + 3 * GDN_W, 3 * NA_W + 4 * GDN_W
    na_ref[...] = _mm(h, w_ref[:, :c0]).astype(BF16)
    qkv_ref[...] = _mm(h, w_ref[:, c0:c1])
    z_ref[...] = _mm(h, w_ref[:, c1:c2])
    gate_ref[...] = _mm(h, w_ref[:, c2:])


def _even_in(x, g, sh, sc, w, n_lat_tiles):
    b, s, d = x.shape
    nt = s // TOK_TILE
    return pl.pallas_call(
        _even_in_body, grid=(b, nt),
        in_specs=[_tok_spec(d), _full_spec((1, d)), _mod_spec(d, n_lat_tiles), _mod_spec(d, n_lat_tiles),
                  _full_spec(w.shape)],
        out_specs=[_tok_spec(3 * NA_W), _tok_spec(3 * GDN_W), _tok_spec(GDN_W), _tok_spec(LANE)],
        out_shape=[jax.ShapeDtypeStruct((b, s, 3 * NA_W), BF16),
                   jax.ShapeDtypeStruct((b, s, 3 * GDN_W), F32),
                   jax.ShapeDtypeStruct((b, s, GDN_W), F32),
                   jax.ShapeDtypeStruct((b, s, LANE), F32)],
        compiler_params=_params("parallel", "parallel"), name="even_in_proj",
    )(x, g, sh, sc, w)


def _softmax_pv(parts):
    m = None
    for s, _ in parts:
        mi = jnp.max(s, axis=-1, keepdims=True)
        m = mi if m is None else jnp.maximum(m, mi)
    l = None
    o = None
    for s, v in parts:
        p = jnp.exp(s - m)
        li = jnp.sum(p, axis=-1, keepdims=True)
        oi = _mm(p.astype(BF16), v)
        l = li if l is None else l + li
        o = oi if o is None else o + oi
    return o / l


def _na_body(q_ref, k_ref, v_ref, bias_ref, o_ref, *, n_lat_tiles, rows, n_lat, n_ctx):
    i = pl.program_id(2)
    lane = lax.broadcasted_iota(I32, (1, LANE), 1)
    lo = lane < NA_HEAD_DIM
    scale = jnp.asarray(NA_HEAD_DIM ** -0.5, BF16)
    kc = k_ref[n_lat:n_lat + n_ctx, :]
    vc = v_ref[n_lat:n_lat + n_ctx, :]
    zero = jnp.zeros((), BF16)
    rows_per_tile = TOK_TILE // GRID_W

    @pl.when(i < n_lat_tiles)
    def _():
        for rr in range(rows_per_tile):
            r = i * rows_per_tile + rr
            r_start = jnp.clip(r - NA_WIN_ROWS // 2, 0, rows - NA_WIN_ROWS)
            delta = r - r_start
            start = pl.multiple_of(r_start * GRID_W, GRID_W)
            kwin = k_ref[pl.ds(start, NA_WIN_ROWS * GRID_W), :]
            vwin = v_ref[pl.ds(start, NA_WIN_ROWS * GRID_W), :]
            q = q_ref[rr * GRID_W:(rr + 1) * GRID_W, :] * scale
            outs = []
            for e, sel in enumerate((lo, jnp.logical_not(lo))):
                qm = jnp.where(sel, q, zero)
                s_lat = _nt(qm, kwin) + bias_ref[delta, e]
                s_ctx = _nt(qm, kc)
                outs.append(_softmax_pv([(s_ctx, vc), (s_lat, vwin)]))
            o_ref[rr * GRID_W:(rr + 1) * GRID_W, :] = jnp.where(lo, outs[0], outs[1])

    @pl.when(i >= n_lat_tiles)
    def _():
        q = q_ref[...] * scale
        outs = []
        for sel in (lo, jnp.logical_not(lo)):
            qm = jnp.where(sel, q, zero)
            outs.append(_softmax_pv([(_nt(qm, kc), vc)]))
        o_ref[...] = jnp.where(lo, outs[0], outs[1])


def _na_bias(rpb):
    wr, wc = NA_WIN_ROWS, NA_WIN_COLS
    q_col = jnp.arange(GRID_W)
    c_start = jnp.clip(q_col - wc // 2, 0, GRID_W - wc)
    k_col = jnp.tile(jnp.arange(GRID_W), wr)
    k_row = jnp.repeat(jnp.arange(wr), GRID_W)
    col_mask = (k_col[None, :] >= c_start[:, None]) & (k_col[None, :] < c_start[:, None] + wc)
    dc_idx = jnp.clip(k_col[None, :] - q_col[:, None], 1 - wc, wc - 1) + (wc - 1)
    delta = jnp.arange(wr)
    dr_idx = k_row[None, :] - delta[:, None] + (wr - 1)
    bias = rpb[:, dr_idx[:, None, :], dc_idx[None, :, :]].astype(F32)
    bias = jnp.where(col_mask[None, None], bias, NEG)
    return bias.transpose(1, 0, 2, 3)


def _na_attention(p_na, bias, n_lat, n_ctx):
    b, s, _ = p_na.shape
    nt = s // TOK_TILE
    pairs = NA_W // LANE
    rows = n_lat // GRID_W
    body = functools.partial(_na_body, n_lat_tiles=n_lat // TOK_TILE, rows=rows, n_lat=n_lat, n_ctx=n_ctx)
    return pl.pallas_call(
        body, grid=(b, pairs, nt),
        in_specs=[pl.BlockSpec((None, TOK_TILE, LANE), lambda bb, hp, i: (bb, i, hp)),
                  pl.BlockSpec((None, s, LANE), lambda bb, hp, i: (bb, 0, pairs + hp)),
                  pl.BlockSpec((None, s, LANE), lambda bb, hp, i: (bb, 0, 2 * pairs + hp)),
                  pl.BlockSpec((NA_WIN_ROWS, 2, GRID_W, NA_WIN_ROWS * GRID_W), lambda bb, hp, i: (0, hp, 0, 0))],
        out_specs=pl.BlockSpec((None, TOK_TILE, LANE), lambda bb, hp, i: (bb, i, hp)),
        out_shape=jax.ShapeDtypeStruct((b, s, NA_W), F32),
        compiler_params=_params("parallel", "parallel", "arbitrary"), name="na_attention",
    )(p_na, p_na, p_na, bias)


def _gdn_prep_body(prev_ref, cur_ref, next_ref, gate_ref, cw_ref, alog_ref, dtb_ref,
                   q_ref, k_ref, v_ref, g_ref, *, n_lat_tiles, nt):
    i = pl.program_id(1)
    top = jnp.where(jnp.logical_or(i == 0, i == n_lat_tiles), 0.0, 1.0)
    bot = jnp.where(jnp.logical_or(i == n_lat_tiles - 1, i == nt - 1), 0.0, 1.0)
    ext = jnp.concatenate([prev_ref[...] * top, cur_ref[...], next_ref[...] * bot], axis=0)
    n = TOK_TILE + 2 * HALO
    y = None
    for j in range(GDN_CONV):
        off = j - GDN_CONV // 2
        shifted = ext if off == 0 else pltpu.roll(ext, (-off) % n, axis=0)
        term = cw_ref[j:j + 1, :] * shifted[HALO:HALO + TOK_TILE, :]
        y = term if y is None else y + term
    y = _silu(y)

    def l2(t):
        return t * lax.rsqrt(jnp.sum(t * t, axis=-1, keepdims=True) + EPS)

    for h in range(GDN_HEADS):
        a, bnd = h * GDN_HEAD_DIM, (h + 1) * GDN_HEAD_DIM
        q_ref[:, a:bnd] = l2(y[:, a:bnd]) * (GDN_HEAD_DIM ** -0.5)
        k_ref[:, a:bnd] = l2(y[:, GDN_W + a:GDN_W + bnd])
    v_ref[...] = y[:, 2 * GDN_W:]

    gr = gate_ref[...]
    lane = lax.broadcasted_iota(I32, (1, LANE), 1)
    is_decay = (lane % (2 * GDN_HEADS)) < GDN_HEADS
    t = gr + dtb_ref[...]
    softplus = jnp.maximum(t, 0.0) + jnp.log1p(jnp.exp(-jnp.abs(t)))
    g_ref[...] = jnp.where(is_decay, -jnp.exp(alog_ref[...]) * softplus, jax.nn.sigmoid(gr))


def _gdn_prep(p_qkv, p_gate, conv_w, alog_row, dtb_row, n_lat_tiles):
    b, s, c = p_qkv.shape
    nt = s // TOK_TILE
    per = TOK_TILE // HALO
    body = functools.partial(_gdn_prep_body, n_lat_tiles=n_lat_tiles, nt=nt)
    return pl.pallas_call(
        body, grid=(b, nt),
        in_specs=[pl.BlockSpec((None, HALO, c), lambda bb, t: (bb, jnp.maximum(t * per - 1, 0), 0)),
                  _tok_spec(c),
                  pl.BlockSpec((None, HALO, c), lambda bb, t: (bb, jnp.minimum((t + 1) * per, s // HALO - 1), 0)),
                  _tok_spec(LANE), _full_spec(conv_w.shape), _full_spec((1, LANE)), _full_spec((1, LANE))],
        out_specs=[_tok_spec(GDN_W), _tok_spec(GDN_W), _tok_spec(GDN_W), _tok_spec(LANE)],
        out_shape=[jax.ShapeDtypeStruct((b, s, GDN_W), F32)] * 3 + [jax.ShapeDtypeStruct((b, s, LANE), F32)],
        compiler_params=_params("parallel", "parallel"), name="gdn_prep",
    )(p_qkv, p_qkv, p_qkv, p_gate, conv_w, alog_row, dtb_row)


def _gdn_chain(q, k, v, gc_col, gc_row, beta_col, tot, state, incl, strict):
    dec = jnp.exp(jnp.where(incl, gc_col - gc_row, NEG))
    a_mat = jnp.where(strict, beta_col * _nt(k, k, HIGHEST) * dec, 0.0)
    eye = jnp.where(jnp.logical_and(incl, jnp.logical_not(strict)), 1.0, 0.0)
    npow = -a_mat
    inv = eye + npow
    for _ in range(5):
        npow = _mm(npow, npow, HIGHEST)
        inv = inv + _mm(inv, npow, HIGHEST)
    egc = jnp.exp(gc_col)
    rhs = jnp.concatenate([beta_col * v, (beta_col * egc) * k], axis=1)
    w = _mm(inv, rhs, HIGHEST)
    w_v, w_k = w[:, :GDN_HEAD_DIM], w[:, GDN_HEAD_DIM:]
    u = w_v - _mm(w_k, state, HIGHEST)
    p_in = _nt(q, k, HIGHEST) * dec
    o = _mm(q * egc, state, HIGHEST) + _mm(p_in, u, HIGHEST)
    k_end = k * jnp.exp(tot - gc_col)
    new_state = jnp.exp(tot) * state + _mm(k_end.T, u, HIGHEST)
    return o, new_state


def _gdn_scan_body(qf, kf, vf, gf, qb, kb, vb, gb, of, ob, s_sc):
    @pl.when(pl.program_id(1) == 0)
    def _():
        s_sc[...] = jnp.zeros_like(s_sc)

    ii = lax.broadcasted_iota(I32, (GDN_CHUNK, GDN_CHUNK), 0)
    jj = lax.broadcasted_iota(I32, (GDN_CHUNK, GDN_CHUNK), 1)
    for d, (q_ref, k_ref, v_ref, g_ref, o_ref) in enumerate(((qf, kf, vf, gf, of), (qb, kb, vb, gb, ob))):
        incl = (ii >= jj) if d == 0 else (ii <= jj)
        strict = (ii > jj) if d == 0 else (ii < jj)
        gates = g_ref[...]
        gcum = _mm(jnp.where(incl, 1.0, 0.0), gates, HIGHEST)
        gcum_t = gcum.T
        gtot = jnp.sum(gates, axis=0, keepdims=True)
        for h in range(GDN_HEADS):
            ca = d * 2 * GDN_HEADS + h
            cb = ca + GDN_HEADS
            sl = slice(h * GDN_HEAD_DIM, (h + 1) * GDN_HEAD_DIM)
            o, new_state = _gdn_chain(
                q_ref[:, sl], k_ref[:, sl], v_ref[:, sl],
                gcum[:, ca:ca + 1], gcum_t[ca:ca + 1, :], gates[:, cb:cb + 1], gtot[:, ca:ca + 1],
                s_sc[d * GDN_HEADS + h], incl, strict)
            s_sc[d * GDN_HEADS + h] = new_state
            o_ref[:, sl] = o


def _gdn_scan(q, k, v, g, n_lat, n_ctx):
    b, s, w = q.shape
    nc = s // GDN_CHUNK
    nlc, ncc = n_lat // GDN_CHUNK, n_ctx // GDN_CHUNK

    def fwd(j):
        return jnp.where(j < ncc, nlc + j, j - ncc)

    def bwd(j):
        return nc - 1 - j

    def spec(width, order):
        return pl.BlockSpec((None, GDN_CHUNK, width), lambda bb, j: (bb, order(j), 0))

    return pl.pallas_call(
        _gdn_scan_body, grid=(b, nc),
        in_specs=[spec(w, fwd), spec(w, fwd), spec(w, fwd), spec(LANE, fwd),
                  spec(w, bwd), spec(w, bwd), spec(w, bwd), spec(LANE, bwd)],
        out_specs=[spec(w, fwd), spec(w, bwd)],
        out_shape=[jax.ShapeDtypeStruct((b, s, w), F32)] * 2,
        scratch_shapes=[pltpu.VMEM((2 * GDN_HEADS, GDN_HEAD_DIM, GDN_HEAD_DIM), F32)],
        compiler_params=_params("parallel", "arbitrary"), name="gdn_scan",
    )(q, k, v, g, q, k, v, g)


def _even_out_body(ya_ref, of_ref, ob_ref, z_ref, ng_ref, w_ref, x_ref, g1_ref, o_ref):
    o = of_ref[...] + ob_ref[...]
    gate = _silu(z_ref[...])
    parts = []
    for h in range(GDN_HEADS):
        sl = slice(h * GDN_HEAD_DIM, (h + 1) * GDN_HEAD_DIM)
        parts.append(_rms(o[:, sl]) * ng_ref[...] * gate[:, sl])
    yb = jnp.concatenate(parts, axis=1).astype(BF16)
    y = _mm(ya_ref[...].astype(BF16), w_ref[:NA_W, :]) + _mm(yb, w_ref[NA_W:, :])
    o_ref[...] = x_ref[...] + g1_ref[...] * y


def _even_out(ya, o_f, o_b, z, norm_g, w_out, x, g1, n_lat_tiles):
    b, s, d = x.shape
    return pl.pallas_call(
        _even_out_body, grid=(b, s // TOK_TILE),
        in_specs=[_tok_spec(NA_W), _tok_spec(GDN_W), _tok_spec(GDN_W), _tok_spec(GDN_W),
                  _full_spec((1, GDN_HEAD_DIM)), _full_spec(w_out.shape), _tok_spec(d), _mod_spec(d, n_lat_tiles)],
        out_specs=_tok_spec(d),
        out_shape=jax.ShapeDtypeStruct((b, s, d), F32),
        compiler_params=_params("parallel", "parallel"), name="even_out_proj",
    )(ya, o_f, o_b, z, norm_g, w_out, x, g1)


def _rot_cols(w):
    a, b, c, d = jnp.split(w, 4, axis=-1)
    return jnp.concatenate([-b, a, -d, c], axis=-1)


def _mla_proj_body(x_ref, g_ref, sh_ref, sc_ref, win_ref, qg_ref, kvg_ref, w1_ref, w2_ref, wkv_ref,
                   cq_ref, sq_ref, ck_ref, sk_ref, q_out, k_out, v_out):
    h = _norm_mod(x_ref[...], g_ref[...], sh_ref[...], sc_ref[...]).astype(BF16)
    p = _mm(h, win_ref[...])
    cq = (_rms(p[:, :MLA_Q_RANK]) * qg_ref[...]).astype(BF16)
    q1 = _mm(cq, w1_ref[...])
    q2 = _mm(cq, w2_ref[...])
    cos_q = jnp.concatenate([cq_ref[...]] * MLA_HEADS, axis=1)
    sin_q = jnp.concatenate([sq_ref[...]] * MLA_HEADS, axis=1)
    q_out[...] = (q1 * cos_q + q2 * sin_q).astype(BF16)
    kv_lo = MLA_Q_RANK + MLA_KV_RANK
    ckv = (_rms(p[:, MLA_Q_RANK:kv_lo]) * kvg_ref[...]).astype(BF16)
    kv = _mm(ckv, wkv_ref[...])
    kr = (p[:, kv_lo:kv_lo + LANE] * ck_ref[...] + p[:, kv_lo + LANE:] * sk_ref[...]).astype(BF16)
    per = MLA_NOPE + MLA_V
    for hh in range(MLA_HEADS):
        k_out[:, hh * MLA_QK_PAD:hh * MLA_QK_PAD + MLA_NOPE] = kv[:, hh * per:hh * per + MLA_NOPE].astype(BF16)
        k_out[:, hh * MLA_QK_PAD + MLA_NOPE:(hh + 1) * MLA_QK_PAD] = kr
        v_out[:, hh * MLA_V:(hh + 1) * MLA_V] = kv[:, hh * per + MLA_NOPE:(hh + 1) * per].astype(BF16)


def _mla_weights(w_in, w_uq, w_ukv):
    d = w_in.shape[0]
    kv_lo = MLA_Q_RANK + MLA_KV_RANK
    kr = w_in[:, kv_lo:]
    zpad = jnp.zeros((d, LANE - MLA_ROPE), F32)
    win_ext = jnp.concatenate([w_in[:, :kv_lo], kr, zpad, _rot_cols(kr), zpad], axis=1).astype(BF16)
    wq = w_uq.reshape(MLA_Q_RANK, MLA_HEADS, MLA_NOPE + MLA_ROPE)
    nope, rope = wq[..., :MLA_NOPE], wq[..., MLA_NOPE:]
    z64 = jnp.zeros((MLA_Q_RANK, MLA_HEADS, MLA_QK_PAD - MLA_NOPE - MLA_ROPE), F32)
    w1 = jnp.concatenate([nope, rope, z64], axis=-1).reshape(MLA_Q_RANK, MLA_HEADS * MLA_QK_PAD).astype(BF16)
    w2 = jnp.concatenate([jnp.zeros_like(nope), _rot_cols(rope), z64], axis=-1)
    w2 = w2.reshape(MLA_Q_RANK, MLA_HEADS * MLA_QK_PAD).astype(BF16)
    return win_ext, w1, w2, w_ukv.astype(BF16)


def _rope_tables(n_lat, n_ctx):
    half = MLA_ROPE // 2
    inv_freq = ROPE_THETA ** (-jnp.arange(0, half, 2, dtype=F32) / half)
    t = jnp.arange(n_lat, dtype=I32)
    row = (t // GRID_W).astype(F32)
    col = (t % GRID_W).astype(F32)
    ang_r = row[:, None] * inv_freq[None, :]
    ang_c = col[:, None] * inv_freq[None, :]
    ang = jnp.concatenate([ang_r, ang_r, ang_c, ang_c], axis=-1)
    cos = jnp.concatenate([jnp.cos(ang), jnp.ones((n_ctx, MLA_ROPE), F32)], axis=0)
    sin = jnp.concatenate([jnp.sin(ang), jnp.zeros((n_ctx, MLA_ROPE), F32)], axis=0)
    s = n_lat + n_ctx
    scale = (MLA_NOPE + MLA_ROPE) ** -0.5
    pad_q = jnp.zeros((s, MLA_QK_PAD - MLA_NOPE - MLA_ROPE), F32)
    cos_q = jnp.concatenate([jnp.ones((s, MLA_NOPE), F32), cos, pad_q], axis=1) * scale
    sin_q = jnp.concatenate([jnp.zeros((s, MLA_NOPE), F32), sin, pad_q], axis=1) * scale
    pad_k = jnp.zeros((s, LANE - MLA_ROPE), F32)
    cos_k = jnp.concatenate([cos, pad_k], axis=1)
    sin_k = jnp.concatenate([sin, pad_k], axis=1)
    return cos_q, sin_q, cos_k, sin_k


def _mla_proj(x, g, sh, sc, weights, q_g, kv_g, tables, n_lat_tiles):
    b, s, d = x.shape
    win_ext, w1, w2, wkv = weights
    cos_q, sin_q, cos_k, sin_k = tables
    tab = lambda width: pl.BlockSpec((TOK_TILE, width), lambda bb, t: (t, 0))
    qw = MLA_HEADS * MLA_QK_PAD
    return pl.pallas_call(
        _mla_proj_body, grid=(b, s // TOK_TILE),
        in_specs=[_tok_spec(d), _full_spec((1, d)), _mod_spec(d, n_lat_tiles), _mod_spec(d, n_lat_tiles),
                  _full_spec(win_ext.shape), _full_spec((1, MLA_Q_RANK)), _full_spec((1, MLA_KV_RANK)),
                  _full_spec(w1.shape), _full_spec(w2.shape), _full_spec(wkv.shape),
                  tab(MLA_QK_PAD), tab(MLA_QK_PAD), tab(LANE), tab(LANE)],
        out_specs=[_tok_spec(qw), _tok_spec(qw), _tok_spec(MLA_HEADS * MLA_V)],
        out_shape=[jax.ShapeDtypeStruct((b, s, qw), BF16), jax.ShapeDtypeStruct((b, s, qw), BF16),
                   jax.ShapeDtypeStruct((b, s, MLA_HEADS * MLA_V), BF16)],
        compiler_params=_params("parallel", "parallel"), name="mla_proj",
    )(x, g, sh, sc, win_ext, q_g, kv_g, w1, w2, wkv, cos_q, sin_q, cos_k, sin_k)


def _flash_body(q_ref, k_ref, v_ref, o_ref, m_sc, l_sc, acc_sc, *, nq, nk, n_ctx):
    qi = pl.program_id(2)
    kj = pl.program_id(3)
    tq, tk = q_ref.shape[0], k_ref.shape[0]

    @pl.when(kj == 0)
    def _():
        m_sc[...] = jnp.full_like(m_sc, NEG)
        l_sc[...] = jnp.zeros_like(l_sc)
        acc_sc[...] = jnp.zeros_like(acc_sc)

    def step(masked):
        s = _nt(q_ref[...], k_ref[...])
        if masked:
            row = lax.broadcasted_iota(I32, (tq, tk), 0)
            col = lax.broadcasted_iota(I32, (tq, tk), 1)
            ctx_key = jnp.logical_and(kj == nk - 1, col >= tk - n_ctx)
            s = jnp.where(jnp.logical_and(row >= tq - n_ctx, jnp.logical_not(ctx_key)), NEG, s)
        m_prev = m_sc[...]
        m_new = jnp.maximum(m_prev, jnp.max(s, axis=-1, keepdims=True))
        alpha = jnp.exp(m_prev - m_new)
        p = jnp.exp(s - m_new)
        l_sc[...] = alpha * l_sc[...] + jnp.sum(p, axis=-1, keepdims=True)
        acc_sc[...] = alpha * acc_sc[...] + _mm(p.astype(BF16), v_ref[...])
        m_sc[...] = m_new

    pl.when(qi == nq - 1)(lambda: step(True))
    pl.when(qi != nq - 1)(lambda: step(False))

    @pl.when(kj == nk - 1)
    def _():
        o_ref[...] = (acc_sc[...] / l_sc[...]).astype(o_ref.dtype)


def _flash_tile(s):
    return 768 if s % 768 == 0 else 256


def _mla_attention(q, k, v, n_ctx):
    b, s, _ = q.shape
    t = _flash_tile(s)
    nq = nk = s // t
    body = functools.partial(_flash_body, nq=nq, nk=nk, n_ctx=n_ctx)
    return pl.pallas_call(
        body, grid=(b, MLA_HEADS, nq, nk),
        in_specs=[pl.BlockSpec((None, t, MLA_QK_PAD), lambda bb, h, i, j: (bb, i, h)),
                  pl.BlockSpec((None, t, MLA_QK_PAD), lambda bb, h, i, j: (bb, j, h)),
                  pl.BlockSpec((None, t, MLA_V), lambda bb, h, i, j: (bb, j, h))],
        out_specs=pl.BlockSpec((None, t, MLA_V), lambda bb, h, i, j: (bb, i, h)),
        out_shape=jax.ShapeDtypeStruct((b, s, MLA_HEADS * MLA_V), BF16),
        scratch_shapes=[pltpu.VMEM((t, 1), F32), pltpu.VMEM((t, 1), F32), pltpu.VMEM((t, MLA_V), F32)],
        compiler_params=_params("parallel", "parallel", "parallel", "arbitrary"), name="mla_flash",
    )(q, k, v)


def _odd_out_body(a_ref, w_ref, x_ref, g1_ref, o_ref):
    o_ref[...] = x_ref[...] + g1_ref[...] * _mm(a_ref[...], w_ref[...])


def _odd_out(a, w_out, x, g1, n_lat_tiles):
    b, s, d = x.shape
    return pl.pallas_call(
        _odd_out_body, grid=(b, s // TOK_TILE),
        in_specs=[_tok_spec(a.shape[-1]), _full_spec(w_out.shape), _tok_spec(d), _mod_spec(d, n_lat_tiles)],
        out_specs=_tok_spec(d),
        out_shape=jax.ShapeDtypeStruct((b, s, d), F32),
        compiler_params=_params("parallel", "parallel"), name="odd_out_proj",
    )(a, w_out, x, g1)


def _top_rows(s, k):
    n = s.shape[0]
    iota = lax.broadcasted_iota(I32, s.shape, 0).astype(F32)
    vals, idxs = [], []
    for _ in range(k):
        m = jnp.max(s, axis=0, keepdims=True)
        i = jnp.min(jnp.where(s == m, iota, float(n)), axis=0, keepdims=True)
        vals.append(m)
        idxs.append(i)
        s = jnp.where(iota == i, -jnp.inf, s)
    return jnp.concatenate(vals, axis=0), jnp.concatenate(idxs, axis=0)


def _peer_route_body(x_ref, g_ref, sh_ref, sc_ref, wq_ref, keys_ref, h_ref, idx_ref, gate_ref):
    h = _norm_mod(x_ref[...], g_ref[...], sh_ref[...], sc_ref[...])
    h_ref[...] = h
    qy = _mm(h.astype(BF16), wq_ref[...]).astype(BF16)
    kk = PEER_TOPK
    t = qy.shape[0]
    idx_rows, gate_rows = [], []
    for hd in range(PEER_HEADS):
        tops = []
        for part in range(2):
            c = (hd * 2 + part) * PEER_KEYS
            s = _nt(keys_ref[hd * 2 + part], qy[:, c:c + PEER_KEYS])
            tops.append(_top_rows(s, kk))
        (s1, i1), (s2, i2) = tops
        comb = (s1[:, None, :] + s2[None, :, :]).reshape(kk * kk, t)
        c_top, c_idx = _top_rows(comb, kk)
        a_sel = jnp.floor(c_idx * (1.0 / kk))
        b_sel = c_idx - a_sel * kk
        e1 = jnp.zeros((kk, t), F32)
        e2 = jnp.zeros((kk, t), F32)
        for r in range(kk):
            e1 = jnp.where(a_sel == r, i1[r:r + 1, :], e1)
            e2 = jnp.where(b_sel == r, i2[r:r + 1, :], e2)
        idx_rows.append(e1 * PEER_KEYS + e2)
        ex = jnp.exp(c_top - c_top[0:1, :])
        gate_rows.append(ex / jnp.sum(ex, axis=0, keepdims=True))
    idx_ref[...] = jnp.concatenate(idx_rows, axis=0).astype(I32)
    gate_ref[...] = jnp.concatenate(gate_rows, axis=0)


def _peer_route(x, g, sh, sc, w_q, keys, n_lat_tiles):
    b, s, d = x.shape
    nt = s // TOK_TILE
    t_spec = pl.BlockSpec((None, None, PEER_SEL, TOK_TILE), lambda bb, t: (bb, t, 0, 0))
    return pl.pallas_call(
        _peer_route_body, grid=(b, nt),
        in_specs=[_tok_spec(d), _full_spec((1, d)), _mod_spec(d, n_lat_tiles), _mod_spec(d, n_lat_tiles),
                  _full_spec(w_q.shape), _full_spec(keys.shape)],
        out_specs=[_tok_spec(d), t_spec, t_spec],
        out_shape=[jax.ShapeDtypeStruct((b, s, d), F32),
                   jax.ShapeDtypeStruct((b, nt, PEER_SEL, TOK_TILE), I32),
                   jax.ShapeDtypeStruct((b, nt, PEER_SEL, TOK_TILE), F32)],
        compiler_params=_params("parallel", "parallel"), name="peer_route",
    )(x, g, sh, sc, w_q, keys)


def _gelu(a):
    return 0.5 * a * (1.0 + lax.erf(a * (2.0 ** -0.5)))


def _peer_expert_body(idx_ref, gate_ref, h_ref, x_ref, g2_ref, u_hbm, v_hbm, o_ref, ubuf, vbuf, sem):
    n = PEER_GROUP * PEER_SEL

    def row_copies(r, e):
        return (pltpu.make_async_copy(u_hbm.at[pl.ds(e, 1)], ubuf.at[pl.ds(r, 1)], sem.at[0]),
                pltpu.make_async_copy(v_hbm.at[pl.ds(e, 1)], vbuf.at[pl.ds(r, 1)], sem.at[1]))

    def issue(r, carry):
        for cp in row_copies(r, idx_ref[0, 0, r]):
            cp.start()
        return carry

    lax.fori_loop(0, n, issue, 0)

    pltpu.make_async_copy(u_hbm.at[pl.ds(0, n)], ubuf, sem.at[0]).wait()
    pltpu.make_async_copy(v_hbm.at[pl.ds(0, n)], vbuf, sem.at[1]).wait()

    rows = lax.broadcasted_iota(I32, (PEER_GROUP, PEER_SEL), 0)
    scores = _nt(h_ref[...].astype(BF16), ubuf[...].astype(BF16))
    a = jnp.zeros((PEER_GROUP, PEER_SEL), F32)
    for t in range(PEER_GROUP):
        a = jnp.where(rows == t, scores[:, t * PEER_SEL:(t + 1) * PEER_SEL], a)
    w = (gate_ref[...] * _gelu(a)).astype(BF16)
    zero = jnp.zeros((), BF16)
    w_diag = jnp.concatenate([jnp.where(rows == t, w, zero) for t in range(PEER_GROUP)], axis=1)
    y = _mm(w_diag, vbuf[...].astype(BF16))
    o_ref[...] = x_ref[...] + g2_ref[...] * y


def _peer_expert(idx, gate, h, x, g2, u_tab, v_tab, n_lat):
    b, s, d = x.shape
    groups = s // PEER_GROUP
    n = PEER_GROUP * PEER_SEL
    lat_groups = n_lat // PEER_GROUP
    tok = lambda width: pl.BlockSpec((None, PEER_GROUP, width), lambda bb, t: (bb, t, 0))
    return pl.pallas_call(
        _peer_expert_body, grid=(b, groups),
        in_specs=[pl.BlockSpec((1, 1, n), lambda bb, t: (bb * groups + t, 0, 0), memory_space=pltpu.SMEM),
                  tok(PEER_SEL), tok(d), tok(d),
                  pl.BlockSpec((None, None, 1, d), lambda bb, t: (bb, jnp.minimum(t // lat_groups, 1), 0, 0)),
                  pl.BlockSpec(memory_space=pl.ANY), pl.BlockSpec(memory_space=pl.ANY)],
        out_specs=tok(d),
        out_shape=jax.ShapeDtypeStruct((b, s, d), F32),
        scratch_shapes=[pltpu.VMEM((n, d), F32), pltpu.VMEM((n, d), F32), pltpu.SemaphoreType.DMA((2,))],
        compiler_params=_params("arbitrary", "arbitrary"), name="peer_expert",
    )(idx.reshape(b * groups, 1, n), gate, h, x, g2, u_tab, v_tab)


def _final_body(x_ref, g_ref, o_ref):
    o_ref[...] = _rms(x_ref[...]) * g_ref[...]


def _final_norm(x, g, n_lat):
    b, s, d = x.shape
    return pl.pallas_call(
        _final_body, grid=(b, n_lat // TOK_TILE),
        in_specs=[_tok_spec(d), _full_spec((1, d))],
        out_specs=_tok_spec(d),
        out_shape=jax.ShapeDtypeStruct((b, n_lat, d), F32),
        compiler_params=_params("parallel", "parallel"), name="final_norm",
    )(x, g)


def _gate_row(vals):
    row = jnp.zeros((2, 2, GDN_HEADS), F32).at[:, 0, :].set(vals.astype(F32)).reshape(1, 4 * GDN_HEADS)
    return jnp.pad(row, ((0, 0), (0, LANE - 4 * GDN_HEADS)))


def kernel(x, c, ctx, c_ctx, ada_w, ada_b, norm1_g, norm2_g, final_g, even_w_in, even_conv_w, gdn_a_log,
           gdn_dt_bias, gdn_norm_g, na_rpb, even_w_out, mla_w_in, mla_q_g, mla_kv_g, mla_w_uq, mla_w_ukv,
           mla_w_out, peer_w_q, peer_sub_keys, peer_u, peer_v):
    bsz, n_lat, d = x.shape
    n_ctx = ctx.shape[1]
    depth = ada_w.shape[0]
    assert n_lat % TOK_TILE == 0 and n_ctx % TOK_TILE == 0 and n_lat // GRID_W >= NA_WIN_ROWS
    assert bsz + 1 <= 2 * HALO
    n_lat_tiles = n_lat // TOK_TILE

    cc = jnp.zeros((2 * HALO, d), F32).at[:bsz].set(c).at[bsz].set(c_ctx)
    mod = _ada(cc, ada_w, ada_b)

    def mods(layer, j):
        lat = mod[layer, :bsz, j * d:(j + 1) * d]
        cx = jnp.broadcast_to(mod[layer, bsz, j * d:(j + 1) * d], (bsz, d))
        return jnp.stack([lat, cx], axis=1)[:, :, None, :]

    xs = jnp.concatenate([x, ctx], axis=1)
    rope = _rope_tables(n_lat, n_ctx)

    for layer in range(depth):
        i = layer // 2
        sh1, sc1, g1, sh2, sc2, g2 = (mods(layer, j) for j in range(6))
        n1 = norm1_g[layer][None, :]
        if layer % 2 == 0:
            w_in = even_w_in[i]
            w_pad = jnp.pad(w_in, ((0, 0), (0, LANE - 4 * GDN_HEADS))).astype(BF16)
            p_na, p_qkv, p_z, p_gate = _even_in(xs, n1, sh1, sc1, w_pad, n_lat_tiles)
            y_a = _na_attention(p_na, _na_bias(na_rpb[i]), n_lat, n_ctx)
            gq, gk, gv, gg = _gdn_prep(p_qkv, p_gate, even_conv_w[i], _gate_row(gdn_a_log[i]),
                                       _gate_row(gdn_dt_bias[i]), n_lat_tiles)
            o_f, o_b = _gdn_scan(gq, gk, gv, gg, n_lat, n_ctx)
            xs = _even_out(y_a, o_f, o_b, p_z, gdn_norm_g[i][None, :], even_w_out[i].astype(BF16), xs, g1,
                           n_lat_tiles)
        else:
            weights = _mla_weights(mla_w_in[i], mla_w_uq[i], mla_w_ukv[i])
            q, k, v = _mla_proj(xs, n1, sh1, sc1, weights, mla_q_g[i][None, :], mla_kv_g[i][None, :], rope,
                                n_lat_tiles)
            att = _mla_attention(q, k, v, n_ctx)
            xs = _odd_out(att, mla_w_out[i].astype(BF16), xs, g1, n_lat_tiles)

        keys = peer_sub_keys[layer].reshape(PEER_HEADS * 2, PEER_KEYS, -1).astype(BF16)
        h2, idx_t, gate_t = _peer_route(xs, norm2_g[layer][None, :], sh2, sc2, peer_w_q[layer].astype(BF16), keys,
                                        n_lat_tiles)
        s = n_lat + n_ctx
        idx = idx_t.transpose(0, 1, 3, 2).reshape(bsz, s, PEER_SEL)
        gate = gate_t.transpose(0, 1, 3, 2).reshape(bsz, s, PEER_SEL)
        xs = _peer_expert(idx, gate, h2, xs, g2, peer_u[layer], peer_v[layer], n_lat)

    return _final_norm(xs, final_g[None, :], n_lat)
```

```python
import functools

import jax
import jax.numpy as jnp
from jax import lax
from jax.experimental import pallas as pl
from jax.experimental.pallas import tpu as pltpu

F32 = jnp.float32
BF16 = jnp.bfloat16
I32 = jnp.int32
HIGHEST = lax.Precision.HIGHEST

EPS = 1e-6
GRID_W = 64
NA_HEAD_DIM = 64
NA_HEADS = 8
NA_WIN_ROWS = 8
NA_WIN_COLS = 16
NA_W = NA_HEADS * NA_HEAD_DIM
GDN_HEAD_DIM = 128
GDN_HEADS = 4
GDN_W = GDN_HEADS * GDN_HEAD_DIM
GDN_CONV = 5
GDN_CHUNK = 64
MLA_HEADS = 8
MLA_NOPE = 128
MLA_ROPE = 64
MLA_V = 128
MLA_Q_RANK = 384
MLA_KV_RANK = 256
MLA_QK_PAD = 256
ROPE_THETA = 10000.0
PEER_KEYS = 128
PEER_HEADS = 8
PEER_TOPK = 16
PEER_SEL = PEER_HEADS * PEER_TOPK

TOK_TILE = 256
LANE = 128
SUBLANE = 8
HALO = SUBLANE
PEER_TILE = 32
PEER_VMEM_LIMIT = 56 * 1024 * 1024
NEG = -1e30
FLASH_Q_SPLIT = 4
VMEM_LIMIT = 56 * 1024 * 1024


def _params(*sem):
    return pltpu.CompilerParams(dimension_semantics=sem, vmem_limit_bytes=VMEM_LIMIT)


def _nt(a, b, precision=None):
    return lax.dot_general(a, b, (((1,), (1,)), ((), ())), precision=precision,
                           preferred_element_type=F32)


def _mm(a, b, precision=None):
    return jnp.dot(a, b, precision=precision, preferred_element_type=F32)


def _rms(x):
    return x * lax.rsqrt(jnp.mean(x * x, axis=-1, keepdims=True) + EPS)


def _norm_mod(x, g, sh, sc):
    return (_rms(x) * g) * (1.0 + sc) + sh


def _silu(x):
    return x * jax.nn.sigmoid(x)


def _ada_body(c_ref, w_ref, b_ref, o_ref):
    o_ref[0] = _mm(_silu(c_ref[...]), w_ref[0], HIGHEST) + b_ref[0]


def _ada(cc, ada_w, ada_b):
    depth, d, n = ada_w.shape
    tn = n // 4
    return pl.pallas_call(
        _ada_body, grid=(depth, n // tn),
        in_specs=[pl.BlockSpec(cc.shape, lambda l, j: (0, 0)),
                  pl.BlockSpec((1, d, tn), lambda l, j: (l, 0, j)),
                  pl.BlockSpec((1, 1, tn), lambda l, j: (l, 0, j))],
        out_specs=pl.BlockSpec((1, cc.shape[0], tn), lambda l, j: (l, 0, j)),
        out_shape=jax.ShapeDtypeStruct((depth, cc.shape[0], n), F32),
        compiler_params=_params("arbitrary", "arbitrary"), name="ada_mod",
    )(cc, ada_w, ada_b.reshape(depth, 1, n))


def _tok_spec(width, col=0):
    return pl.BlockSpec((None, TOK_TILE, width), lambda b, t: (b, t, col))


def _mod_spec(d, n_lat_tiles):
    return pl.BlockSpec((None, None, 1, d), lambda b, t: (b, jnp.minimum(t // n_lat_tiles, 1), 0, 0))


def _full_spec(shape):
    nd = len(shape)
    return pl.BlockSpec(shape, lambda b, t: (0,) * nd)


def _even_in_body(x_ref, g_ref, sh_ref, sc_ref, w_ref, na_ref, qkv_ref, z_ref, gate_ref):
    h = _norm_mod(x_ref[...], g_ref[...], sh_ref[...], sc_ref[...]).astype(BF16)
    c0, c1, c2 = 3 * NA_W, 3 * NA_W + 3 * GDN_W, 3 * NA_W + 4 * GDN_W
    na_ref[...] = _mm(h, w_ref[:, :c0]).astype(BF16)
    qkv_ref[...] = _mm(h, w_ref[:, c0:c1])
    z_ref[...] = _mm(h, w_ref[:, c1:c2])
    gate_ref[...] = _mm(h, w_ref[:, c2:])


def _even_in(x, g, sh, sc, w, n_lat_tiles):
    b, s, d = x.shape
    nt = s // TOK_TILE
    return pl.pallas_call(
        _even_in_body, grid=(b, nt),
        in_specs=[_tok_spec(d), _full_spec((1, d)), _mod_spec(d, n_lat_tiles), _mod_spec(d, n_lat_tiles),
                  _full_spec(w.shape)],
        out_specs=[_tok_spec(3 * NA_W), _tok_spec(3 * GDN_W), _tok_spec(GDN_W), _tok_spec(LANE)],
        out_shape=[jax.ShapeDtypeStruct((b, s, 3 * NA_W), BF16),
                   jax.ShapeDtypeStruct((b, s, 3 * GDN_W), F32),
                   jax.ShapeDtypeStruct((b, s, GDN_W), F32),
                   jax.ShapeDtypeStruct((b, s, LANE), F32)],
        compiler_params=_params("parallel", "parallel"), name="even_in_proj",
    )(x, g, sh, sc, w)


def _softmax_pv(parts):
    m = None
    for s, _ in parts:
        mi = jnp.max(s, axis=-1, keepdims=True)
        m = mi if m is None else jnp.maximum(m, mi)
    l = None
    o = None
    for s, v in parts:
        p = jnp.exp(s - m)
        li = jnp.sum(p, axis=-1, keepdims=True)
        oi = _mm(p.astype(BF16), v)
        l = li if l is None else l + li
        o = oi if o is None else o + oi
    return o / l


def _na_body(q_ref, k_ref, v_ref, bias_ref, o_ref, *, n_lat_tiles, rows, n_lat, n_ctx):
    i = pl.program_id(2)
    lane = lax.broadcasted_iota(I32, (1, LANE), 1)
    lo = lane < NA_HEAD_DIM
    scale = jnp.asarray(NA_HEAD_DIM ** -0.5, BF16)
    kc = k_ref[n_lat:n_lat + n_ctx, :]
    vc = v_ref[n_lat:n_lat + n_ctx, :]
    zero = jnp.zeros((), BF16)
    rows_per_tile = TOK_TILE // GRID_W

    @pl.when(i < n_lat_tiles)
    def _():
        for rr in range(rows_per_tile):
            r = i * rows_per_tile + rr
            r_start = jnp.clip(r - NA_WIN_ROWS // 2, 0, rows - NA_WIN_ROWS)
            delta = r - r_start
            start = pl.multiple_of(r_start * GRID_W, GRID_W)
            kwin = k_ref[pl.ds(start, NA_WIN_ROWS * GRID_W), :]
            vwin = v_ref[pl.ds(start, NA_WIN_ROWS * GRID_W), :]
            q = q_ref[rr * GRID_W:(rr + 1) * GRID_W, :] * scale
            outs = []
            for e, sel in enumerate((lo, jnp.logical_not(lo))):
                qm = jnp.where(sel, q, zero)
                s_lat = _nt(qm, kwin) + bias_ref[delta, e]
                s_ctx = _nt(qm, kc)
                outs.append(_softmax_pv([(s_ctx, vc), (s_lat, vwin)]))
            o_ref[rr * GRID_W:(rr + 1) * GRID_W, :] = jnp.where(lo, outs[0], outs[1])

    @pl.when(i >= n_lat_tiles)
    def _():
        q = q_ref[...] * scale
        outs = []
        for sel in (lo, jnp.logical_not(lo)):
            qm = jnp.where(sel, q, zero)
            outs.append(_softmax_pv([(_nt(qm, kc), vc)]))
        o_ref[...] = jnp.where(lo, outs[0], outs[1])


def _na_bias(rpb):
    wr, wc = NA_WIN_ROWS, NA_WIN_COLS
    q_col = jnp.arange(GRID_W)
    c_start = jnp.clip(q_col - wc // 2, 0, GRID_W - wc)
    k_col = jnp.tile(jnp.arange(GRID_W), wr)
    k_row = jnp.repeat(jnp.arange(wr), GRID_W)
    col_mask = (k_col[None, :] >= c_start[:, None]) & (k_col[None, :] < c_start[:, None] + wc)
    dc_idx = jnp.clip(k_col[None, :] - q_col[:, None], 1 - wc, wc - 1) + (wc - 1)
    delta = jnp.arange(wr)
    dr_idx = k_row[None, :] - delta[:, None] + (wr - 1)
    oh_r = jax.nn.one_hot(dr_idx, 2 * wr - 1, dtype=F32)
    oh_c = jax.nn.one_hot(dc_idx, 2 * wc - 1, dtype=F32)
    bias = jnp.einsum('dkr,hrc,qkc->hdqk', oh_r, rpb.astype(F32), oh_c, precision=HIGHEST)
    bias = jnp.where(col_mask[None, None], bias, NEG)
    return bias.transpose(1, 0, 2, 3)


def _na_attention(p_na, bias, n_lat, n_ctx):
    b, s, _ = p_na.shape
    nt = s // TOK_TILE
    pairs = NA_W // LANE
    rows = n_lat // GRID_W
    body = functools.partial(_na_body, n_lat_tiles=n_lat // TOK_TILE, rows=rows, n_lat=n_lat, n_ctx=n_ctx)
    return pl.pallas_call(
        body, grid=(b, pairs, nt),
        in_specs=[pl.BlockSpec((None, TOK_TILE, LANE), lambda bb, hp, i: (bb, i, hp)),
                  pl.BlockSpec((None, s, LANE), lambda bb, hp, i: (bb, 0, pairs + hp)),
                  pl.BlockSpec((None, s, LANE), lambda bb, hp, i: (bb, 0, 2 * pairs + hp)),
                  pl.BlockSpec((NA_WIN_ROWS, 2, GRID_W, NA_WIN_ROWS * GRID_W), lambda bb, hp, i: (0, hp, 0, 0))],
        out_specs=pl.BlockSpec((None, TOK_TILE, LANE), lambda bb, hp, i: (bb, i, hp)),
        out_shape=jax.ShapeDtypeStruct((b, s, NA_W), F32),
        compiler_params=_params("parallel", "parallel", "arbitrary"), name="na_attention",
    )(p_na, p_na, p_na, bias)


def _gdn_prep_body(prev_ref, cur_ref, next_ref, gate_ref, cw_ref, alog_ref, dtb_ref,
                   q_ref, k_ref, v_ref, g_ref, *, n_lat_tiles, nt):
    i = pl.program_id(1)
    top = jnp.where(jnp.logical_or(i == 0, i == n_lat_tiles), 0.0, 1.0)
    bot = jnp.where(jnp.logical_or(i == n_lat_tiles - 1, i == nt - 1), 0.0, 1.0)
    ext = jnp.concatenate([prev_ref[...] * top, cur_ref[...], next_ref[...] * bot], axis=0)
    n = TOK_TILE + 2 * HALO
    y = None
    for j in range(GDN_CONV):
        off = j - GDN_CONV // 2
        shifted = ext if off == 0 else pltpu.roll(ext, (-off) % n, axis=0)
        term = cw_ref[j:j + 1, :] * shifted[HALO:HALO + TOK_TILE, :]
        y = term if y is None else y + term
    y = _silu(y)

    def l2(t):
        return t * lax.rsqrt(jnp.sum(t * t, axis=-1, keepdims=True) + EPS)

    for h in range(GDN_HEADS):
        a, bnd = h * GDN_HEAD_DIM, (h + 1) * GDN_HEAD_DIM
        q_ref[:, a:bnd] = l2(y[:, a:bnd]) * (GDN_HEAD_DIM ** -0.5)
        k_ref[:, a:bnd] = l2(y[:, GDN_W + a:GDN_W + bnd])
    v_ref[...] = y[:, 2 * GDN_W:]

    gr = gate_ref[...]
    lane = lax.broadcasted_iota(I32, (1, LANE), 1)
    is_decay = (lane % (2 * GDN_HEADS)) < GDN_HEADS
    t = gr + dtb_ref[...]
    softplus = jnp.maximum(t, 0.0) + jnp.log1p(jnp.exp(-jnp.abs(t)))
    g_ref[...] = jnp.where(is_decay, -jnp.exp(alog_ref[...]) * softplus, jax.nn.sigmoid(gr))


def _gdn_prep(p_qkv, p_gate, conv_w, alog_row, dtb_row, n_lat_tiles):
    b, s, c = p_qkv.shape
    nt = s // TOK_TILE
    per = TOK_TILE // HALO
    body = functools.partial(_gdn_prep_body, n_lat_tiles=n_lat_tiles, nt=nt)
    return pl.pallas_call(
        body, grid=(b, nt),
        in_specs=[pl.BlockSpec((None, HALO, c), lambda bb, t: (bb, jnp.maximum(t * per - 1, 0), 0)),
                  _tok_spec(c),
                  pl.BlockSpec((None, HALO, c), lambda bb, t: (bb, jnp.minimum((t + 1) * per, s // HALO - 1), 0)),
                  _tok_spec(LANE), _full_spec(conv_w.shape), _full_spec((1, LANE)), _full_spec((1, LANE))],
        out_specs=[_tok_spec(GDN_W), _tok_spec(GDN_W), _tok_spec(GDN_W), _tok_spec(LANE)],
        out_shape=[jax.ShapeDtypeStruct((b, s, GDN_W), F32)] * 3 + [jax.ShapeDtypeStruct((b, s, LANE), F32)],
        compiler_params=_params("parallel", "parallel"), name="gdn_prep",
    )(p_qkv, p_qkv, p_qkv, p_gate, conv_w, alog_row, dtb_row)


def _split(x):
    hi = x.astype(BF16)
    return hi, (x - hi.astype(F32)).astype(BF16)


def _mm3(a, b):
    ah, al = _split(a)
    bh, bl = _split(b)
    return _mm(ah, bh) + (_mm(ah, bl) + _mm(al, bh))


def _gdn_chunk_step(chains, states):
    n = len(chains)
    kb = [c[1].astype(BF16) for c in chains]
    dec = [jnp.exp(jnp.where(c[7], c[3] - c[4], NEG)) for c in chains]
    kk = [_nt(x, x) for x in kb]
    npow = [-jnp.where(c[8], c[5] * kk[i] * dec[i], 0.0) for i, c in enumerate(chains)]
    inv = [jnp.where(jnp.logical_and(c[7], jnp.logical_not(c[8])), 1.0, 0.0) + npow[i] for i, c in enumerate(chains)]
    for _ in range(5):
        npow = [_mm3(x, x) for x in npow]
        inv = [inv[i] + _mm3(inv[i], npow[i]) for i in range(n)]
    egc = [jnp.exp(c[3]) for c in chains]
    rhs = [jnp.concatenate([c[5] * c[2], (c[5] * egc[i]) * c[1]], axis=1) for i, c in enumerate(chains)]
    w = [_mm3(inv[i], rhs[i]) for i in range(n)]
    sb = [x.astype(BF16) for x in states]
    u = [w[i][:, :GDN_HEAD_DIM] - _mm(w[i][:, GDN_HEAD_DIM:].astype(BF16), sb[i]) for i in range(n)]
    ub = [x.astype(BF16) for x in u]
    p_in = [(_nt(c[0].astype(BF16), kb[i]) * dec[i]).astype(BF16) for i, c in enumerate(chains)]
    o = [_mm((c[0] * egc[i]).astype(BF16), sb[i]) + _mm(p_in[i], ub[i]) for i, c in enumerate(chains)]
    k_end_t = [(c[1] * jnp.exp(c[6] - c[3])).T.astype(BF16) for c in chains]
    new_states = [jnp.exp(c[6]) * states[i] + _mm(k_end_t[i], ub[i]) for i, c in enumerate(chains)]
    return o, new_states


def _gdn_scan_body(qf, kf, vf, gf, qb, kb, vb, gb, of, ob, s_sc):
    @pl.when(pl.program_id(1) == 0)
    def _():
        s_sc[...] = jnp.zeros_like(s_sc)

    ii = lax.broadcasted_iota(I32, (GDN_CHUNK, GDN_CHUNK), 0)
    jj = lax.broadcasted_iota(I32, (GDN_CHUNK, GDN_CHUNK), 1)
    chains = []
    for d, (q_ref, k_ref, v_ref, g_ref) in enumerate(((qf, kf, vf, gf), (qb, kb, vb, gb))):
        incl = (ii >= jj) if d == 0 else (ii <= jj)
        strict = (ii > jj) if d == 0 else (ii < jj)
        gates = g_ref[...]
        gcum = _mm(jnp.where(incl, 1.0, 0.0), gates, HIGHEST)
        gcum_t = gcum.T
        gtot = jnp.sum(gates, axis=0, keepdims=True)
        for h in range(GDN_HEADS):
            ca = d * 2 * GDN_HEADS + h
            cb = ca + GDN_HEADS
            sl = slice(h * GDN_HEAD_DIM, (h + 1) * GDN_HEAD_DIM)
            chains.append((q_ref[:, sl], k_ref[:, sl], v_ref[:, sl], gcum[:, ca:ca + 1], gcum_t[ca:ca + 1, :],
                           gates[:, cb:cb + 1], gtot[:, ca:ca + 1], incl, strict))
    outs, new_states = _gdn_chunk_step(chains, [s_sc[i] for i in range(2 * GDN_HEADS)])
    for i in range(2 * GDN_HEADS):
        s_sc[i] = new_states[i]
        o_ref = of if i < GDN_HEADS else ob
        h = i % GDN_HEADS
        o_ref[:, h * GDN_HEAD_DIM:(h + 1) * GDN_HEAD_DIM] = outs[i]


def _gdn_scan(q, k, v, g, n_lat, n_ctx):
    b, s, w = q.shape
    nc = s // GDN_CHUNK
    nlc, ncc = n_lat // GDN_CHUNK, n_ctx // GDN_CHUNK

    def fwd(j):
        return jnp.where(j < ncc, nlc + j, j - ncc)

    def bwd(j):
        return nc - 1 - j

    def spec(width, order):
        return pl.BlockSpec((None, GDN_CHUNK, width), lambda bb, j: (bb, order(j), 0))

    return pl.pallas_call(
        _gdn_scan_body, grid=(b, nc),
        in_specs=[spec(w, fwd), spec(w, fwd), spec(w, fwd), spec(LANE, fwd),
                  spec(w, bwd), spec(w, bwd), spec(w, bwd), spec(LANE, bwd)],
        out_specs=[spec(w, fwd), spec(w, bwd)],
        out_shape=[jax.ShapeDtypeStruct((b, s, w), F32)] * 2,
        scratch_shapes=[pltpu.VMEM((2 * GDN_HEADS, GDN_HEAD_DIM, GDN_HEAD_DIM), F32)],
        compiler_params=_params("parallel", "arbitrary"), name="gdn_scan",
    )(q, k, v, g, q, k, v, g)


def _even_out_body(ya_ref, of_ref, ob_ref, z_ref, ng_ref, w_ref, x_ref, g1_ref, o_ref):
    o = of_ref[...] + ob_ref[...]
    gate = _silu(z_ref[...])
    parts = []
    for h in range(GDN_HEADS):
        sl = slice(h * GDN_HEAD_DIM, (h + 1) * GDN_HEAD_DIM)
        parts.append(_rms(o[:, sl]) * ng_ref[...] * gate[:, sl])
    yb = jnp.concatenate(parts, axis=1).astype(BF16)
    y = _mm(ya_ref[...].astype(BF16), w_ref[:NA_W, :]) + _mm(yb, w_ref[NA_W:, :])
    o_ref[...] = x_ref[...] + g1_ref[...] * y


def _even_out(ya, o_f, o_b, z, norm_g, w_out, x, g1, n_lat_tiles):
    b, s, d = x.shape
    return pl.pallas_call(
        _even_out_body, grid=(b, s // TOK_TILE),
        in_specs=[_tok_spec(NA_W), _tok_spec(GDN_W), _tok_spec(GDN_W), _tok_spec(GDN_W),
                  _full_spec((1, GDN_HEAD_DIM)), _full_spec(w_out.shape), _tok_spec(d), _mod_spec(d, n_lat_tiles)],
        out_specs=_tok_spec(d),
        out_shape=jax.ShapeDtypeStruct((b, s, d), F32),
        compiler_params=_params("parallel", "parallel"), name="even_out_proj",
    )(ya, o_f, o_b, z, norm_g, w_out, x, g1)


def _rot_cols(w):
    a, b, c, d = jnp.split(w, 4, axis=-1)
    return jnp.concatenate([-b, a, -d, c], axis=-1)


def _mla_proj_body(x_ref, g_ref, sh_ref, sc_ref, win_ref, qg_ref, kvg_ref, w1_ref, w2_ref, wkv_ref,
                   cq_ref, sq_ref, ck_ref, sk_ref, q_out, k_out, v_out):
    h = _norm_mod(x_ref[...], g_ref[...], sh_ref[...], sc_ref[...]).astype(BF16)
    p = _mm(h, win_ref[...])
    cq = (_rms(p[:, :MLA_Q_RANK]) * qg_ref[...]).astype(BF16)
    q1 = _mm(cq, w1_ref[...])
    q2 = _mm(cq, w2_ref[...])
    cos_q = jnp.concatenate([cq_ref[...]] * MLA_HEADS, axis=1)
    sin_q = jnp.concatenate([sq_ref[...]] * MLA_HEADS, axis=1)
    q_out[...] = (q1 * cos_q + q2 * sin_q).astype(BF16)
    kv_lo = MLA_Q_RANK + MLA_KV_RANK
    ckv = (_rms(p[:, MLA_Q_RANK:kv_lo]) * kvg_ref[...]).astype(BF16)
    kv = _mm(ckv, wkv_ref[...])
    kr = (p[:, kv_lo:kv_lo + LANE] * ck_ref[...] + p[:, kv_lo + LANE:] * sk_ref[...]).astype(BF16)
    per = MLA_NOPE + MLA_V
    for hh in range(MLA_HEADS):
        k_out[:, hh * MLA_QK_PAD:hh * MLA_QK_PAD + MLA_NOPE] = kv[:, hh * per:hh * per + MLA_NOPE].astype(BF16)
        k_out[:, hh * MLA_QK_PAD + MLA_NOPE:(hh + 1) * MLA_QK_PAD] = kr
        v_out[:, hh * MLA_V:(hh + 1) * MLA_V] = kv[:, hh * per + MLA_NOPE:(hh + 1) * per].astype(BF16)


def _mla_weights(w_in, w_uq, w_ukv):
    d = w_in.shape[0]
    kv_lo = MLA_Q_RANK + MLA_KV_RANK
    kr = w_in[:, kv_lo:]
    zpad = jnp.zeros((d, LANE - MLA_ROPE), F32)
    win_ext = jnp.concatenate([w_in[:, :kv_lo], kr, zpad, _rot_cols(kr), zpad], axis=1).astype(BF16)
    wq = w_uq.reshape(MLA_Q_RANK, MLA_HEADS, MLA_NOPE + MLA_ROPE)
    nope, rope = wq[..., :MLA_NOPE], wq[..., MLA_NOPE:]
    z64 = jnp.zeros((MLA_Q_RANK, MLA_HEADS, MLA_QK_PAD - MLA_NOPE - MLA_ROPE), F32)
    w1 = jnp.concatenate([nope, rope, z64], axis=-1).reshape(MLA_Q_RANK, MLA_HEADS * MLA_QK_PAD).astype(BF16)
    w2 = jnp.concatenate([jnp.zeros_like(nope), _rot_cols(rope), z64], axis=-1)
    w2 = w2.reshape(MLA_Q_RANK, MLA_HEADS * MLA_QK_PAD).astype(BF16)
    return win_ext, w1, w2, w_ukv.astype(BF16)


def _rope_tables(n_lat, n_ctx):
    half = MLA_ROPE // 2
    inv_freq = ROPE_THETA ** (-jnp.arange(0, half, 2, dtype=F32) / half)
    t = jnp.arange(n_lat, dtype=I32)
    row = (t // GRID_W).astype(F32)
    col = (t % GRID_W).astype(F32)
    ang_r = row[:, None] * inv_freq[None, :]
    ang_c = col[:, None] * inv_freq[None, :]
    ang = jnp.concatenate([ang_r, ang_r, ang_c, ang_c], axis=-1)
    cos = jnp.concatenate([jnp.cos(ang), jnp.ones((n_ctx, MLA_ROPE), F32)], axis=0)
    sin = jnp.concatenate([jnp.sin(ang), jnp.zeros((n_ctx, MLA_ROPE), F32)], axis=0)
    s = n_lat + n_ctx
    scale = (MLA_NOPE + MLA_ROPE) ** -0.5
    pad_q = jnp.zeros((s, MLA_QK_PAD - MLA_NOPE - MLA_ROPE), F32)
    cos_q = jnp.concatenate([jnp.ones((s, MLA_NOPE), F32), cos, pad_q], axis=1) * scale
    sin_q = jnp.concatenate([jnp.zeros((s, MLA_NOPE), F32), sin, pad_q], axis=1) * scale
    pad_k = jnp.zeros((s, LANE - MLA_ROPE), F32)
    cos_k = jnp.concatenate([cos, pad_k], axis=1)
    sin_k = jnp.concatenate([sin, pad_k], axis=1)
    return cos_q, sin_q, cos_k, sin_k


def _mla_proj(x, g, sh, sc, weights, q_g, kv_g, tables, n_lat_tiles):
    b, s, d = x.shape
    win_ext, w1, w2, wkv = weights
    cos_q, sin_q, cos_k, sin_k = tables
    tab = lambda width: pl.BlockSpec((TOK_TILE, width), lambda bb, t: (t, 0))
    qw = MLA_HEADS * MLA_QK_PAD
    return pl.pallas_call(
        _mla_proj_body, grid=(b, s // TOK_TILE),
        in_specs=[_tok_spec(d), _full_spec((1, d)), _mod_spec(d, n_lat_tiles), _mod_spec(d, n_lat_tiles),
                  _full_spec(win_ext.shape), _full_spec((1, MLA_Q_RANK)), _full_spec((1, MLA_KV_RANK)),
                  _full_spec(w1.shape), _full_spec(w2.shape), _full_spec(wkv.shape),
                  tab(MLA_QK_PAD), tab(MLA_QK_PAD), tab(LANE), tab(LANE)],
        out_specs=[_tok_spec(qw), _tok_spec(qw), _tok_spec(MLA_HEADS * MLA_V)],
        out_shape=[jax.ShapeDtypeStruct((b, s, qw), BF16), jax.ShapeDtypeStruct((b, s, qw), BF16),
                   jax.ShapeDtypeStruct((b, s, MLA_HEADS * MLA_V), BF16)],
        compiler_params=_params("parallel", "parallel"), name="mla_proj",
    )(x, g, sh, sc, win_ext, q_g, kv_g, w1, w2, wkv, cos_q, sin_q, cos_k, sin_k)


def _flash_body(q_ref, k_ref, v_ref, o_ref, m_sc, l_sc, acc_sc, *, nq, nk, n_ctx):
    qi = pl.program_id(2)
    kj = pl.program_id(3)
    tq, tk = q_ref.shape[0], k_ref.shape[0]

    @pl.when(kj == 0)
    def _():
        m_sc[...] = jnp.full_like(m_sc, NEG)
        l_sc[...] = jnp.zeros_like(l_sc)
        acc_sc[...] = jnp.zeros_like(acc_sc)

    def step(masked):
        k = k_ref[...]
        v = v_ref[...]
        n_sub = FLASH_Q_SPLIT if tq % (FLASH_Q_SPLIT * 2 * SUBLANE) == 0 else 1
        rows = tq // n_sub
        subs = [slice(i * rows, (i + 1) * rows) for i in range(n_sub)]
        s = [_nt(q_ref[r, :], k) for r in subs]
        if masked:
            col = lax.broadcasted_iota(I32, (rows, tk), 1)
            ctx_key = jnp.logical_and(kj == nk - 1, col >= tk - n_ctx)
            for i in range(n_sub):
                row = lax.broadcasted_iota(I32, (rows, tk), 0) + i * rows
                s[i] = jnp.where(jnp.logical_and(row >= tq - n_ctx, jnp.logical_not(ctx_key)), NEG, s[i])
        m_prev = [m_sc[r, :] for r in subs]
        m_new = [jnp.maximum(m_prev[i], jnp.max(s[i], axis=-1, keepdims=True)) for i in range(n_sub)]
        alpha = [jnp.exp(m_prev[i] - m_new[i]) for i in range(n_sub)]
        p = [jnp.exp(s[i] - m_new[i]) for i in range(n_sub)]
        for i, r in enumerate(subs):
            l_sc[r, :] = alpha[i] * l_sc[r, :] + jnp.sum(p[i], axis=-1, keepdims=True)
            acc_sc[r, :] = alpha[i] * acc_sc[r, :] + _mm(p[i].astype(BF16), v)
            m_sc[r, :] = m_new[i]

    pl.when(qi == nq - 1)(lambda: step(True))
    pl.when(qi != nq - 1)(lambda: step(False))

    @pl.when(kj == nk - 1)
    def _():
        o_ref[...] = (acc_sc[...] / l_sc[...]).astype(o_ref.dtype)


def _flash_tile(s):
    return 768 if s % 768 == 0 else 256


def _mla_attention(q, k, v, n_ctx):
    b, s, _ = q.shape
    t = _flash_tile(s)
    nq = nk = s // t
    body = functools.partial(_flash_body, nq=nq, nk=nk, n_ctx=n_ctx)
    return pl.pallas_call(
        body, grid=(b, MLA_HEADS, nq, nk),
        in_specs=[pl.BlockSpec((None, t, MLA_QK_PAD), lambda bb, h, i, j: (bb, i, h)),
                  pl.BlockSpec((None, t, MLA_QK_PAD), lambda bb, h, i, j: (bb, j, h)),
                  pl.BlockSpec((None, t, MLA_V), lambda bb, h, i, j: (bb, j, h))],
        out_specs=pl.BlockSpec((None, t, MLA_V), lambda bb, h, i, j: (bb, i, h)),
        out_shape=jax.ShapeDtypeStruct((b, s, MLA_HEADS * MLA_V), BF16),
        scratch_shapes=[pltpu.VMEM((t, 1), F32), pltpu.VMEM((t, 1), F32), pltpu.VMEM((t, MLA_V), F32)],
        compiler_params=_params("parallel", "parallel", "parallel", "arbitrary"), name="mla_flash",
    )(q, k, v)


def _odd_out_body(a_ref, w_ref, x_ref, g1_ref, o_ref):
    o_ref[...] = x_ref[...] + g1_ref[...] * _mm(a_ref[...], w_ref[...])


def _odd_out(a, w_out, x, g1, n_lat_tiles):
    b, s, d = x.shape
    return pl.pallas_call(
        _odd_out_body, grid=(b, s // TOK_TILE),
        in_specs=[_tok_spec(a.shape[-1]), _full_spec(w_out.shape), _tok_spec(d), _mod_spec(d, n_lat_tiles)],
        out_specs=_tok_spec(d),
        out_shape=jax.ShapeDtypeStruct((b, s, d), F32),
        compiler_params=_params("parallel", "parallel"), name="odd_out_proj",
    )(a, w_out, x, g1)


def _top_rows(s, k):
    n = s.shape[0]
    iota = lax.broadcasted_iota(I32, s.shape, 0).astype(F32)
    vals, idxs = [], []
    for _ in range(k):
        m = jnp.max(s, axis=0, keepdims=True)
        i = jnp.min(jnp.where(s == m, iota, float(n)), axis=0, keepdims=True)
        vals.append(m)
        idxs.append(i)
        s = jnp.where(iota == i, -jnp.inf, s)
    return jnp.concatenate(vals, axis=0), jnp.concatenate(idxs, axis=0)


def _peer_route_body(x_ref, g_ref, sh_ref, sc_ref, wq_ref, keys_ref, h_ref, idx_ref, gate_ref):
    h = _norm_mod(x_ref[...], g_ref[...], sh_ref[...], sc_ref[...])
    h_ref[...] = h
    qy = _mm(h.astype(BF16), wq_ref[...]).astype(BF16)
    kk = PEER_TOPK
    t = qy.shape[0]
    idx_rows, gate_rows = [], []
    for hd in range(PEER_HEADS):
        tops = []
        for part in range(2):
            c = (hd * 2 + part) * PEER_KEYS
            s = _nt(keys_ref[hd * 2 + part], qy[:, c:c + PEER_KEYS])
            tops.append(_top_rows(s, kk))
        (s1, i1), (s2, i2) = tops
        comb = (s1[:, None, :] + s2[None, :, :]).reshape(kk * kk, t)
        c_top, c_idx = _top_rows(comb, kk)
        a_sel = jnp.floor(c_idx * (1.0 / kk))
        b_sel = c_idx - a_sel * kk
        e1 = jnp.zeros((kk, t), F32)
        e2 = jnp.zeros((kk, t), F32)
        for r in range(kk):
            e1 = jnp.where(a_sel == r, i1[r:r + 1, :], e1)
            e2 = jnp.where(b_sel == r, i2[r:r + 1, :], e2)
        idx_rows.append(e1 * PEER_KEYS + e2)
        ex = jnp.exp(c_top - c_top[0:1, :])
        gate_rows.append(ex / jnp.sum(ex, axis=0, keepdims=True))
    idx_ref[...] = jnp.concatenate(idx_rows, axis=0).astype(I32)
    gate_ref[...] = jnp.concatenate(gate_rows, axis=0)


def _peer_route(x, g, sh, sc, w_q, keys, n_lat_tiles):
    b, s, d = x.shape
    nt = s // TOK_TILE
    t_spec = pl.BlockSpec((None, None, PEER_SEL, TOK_TILE), lambda bb, t: (bb, t, 0, 0))
    return pl.pallas_call(
        _peer_route_body, grid=(b, nt),
        in_specs=[_tok_spec(d), _full_spec((1, d)), _mod_spec(d, n_lat_tiles), _mod_spec(d, n_lat_tiles),
                  _full_spec(w_q.shape), _full_spec(keys.shape)],
        out_specs=[_tok_spec(d), t_spec, t_spec],
        out_shape=[jax.ShapeDtypeStruct((b, s, d), F32),
                   jax.ShapeDtypeStruct((b, nt, PEER_SEL, TOK_TILE), I32),
                   jax.ShapeDtypeStruct((b, nt, PEER_SEL, TOK_TILE), F32)],
        compiler_params=_params("parallel", "parallel"), name="peer_route",
    )(x, g, sh, sc, w_q, keys)


def _gelu(a):
    return 0.5 * a * (1.0 + lax.erf(a * (2.0 ** -0.5)))


def _pack_table(tab):
    e, d = tab.shape
    assert d == SUBLANE * LANE and e % 2 == 0
    bits = lax.bitcast_convert_type(tab.astype(BF16), jnp.uint16).astype(jnp.uint32)
    packed = (bits[:e // 2] << 16) | bits[e // 2:]
    return lax.bitcast_convert_type(packed, I32).reshape(e // 2 * SUBLANE, LANE)


HIGH_HALF = -65536


def _table_tile(tab_ref, off):
    return tab_ref[pl.ds(pl.multiple_of(off, SUBLANE), SUBLANE), :]


def _as_f32(bits):
    return lax.bitcast_convert_type(bits, F32)


def _load_table_once(tab_hbm, tab, sem):
    @pl.when(pl.program_id(0) == 0)
    def _():
        cp = pltpu.make_async_copy(tab_hbm, tab, sem)
        cp.start()
        cp.wait()


def _pair_sums(a, b, k, sub):
    take_a = (sub % (2 * k)) < k
    return jnp.where(take_a, a + pltpu.roll(a, SUBLANE - k, axis=0), b + pltpu.roll(b, k, axis=0))


_BUTTERFLY_ORDER = (0, 4, 2, 6, 1, 5, 3, 7)


def _sublane_sums(p, sub):
    l1 = [_pair_sums(p[2 * i], p[2 * i + 1], 4, sub) for i in range(4)]
    l2 = [_pair_sums(l1[0], l1[1], 2, sub), _pair_sums(l1[2], l1[3], 2, sub)]
    return _pair_sums(l2[0], l2[1], 1, sub)


def _eye():
    ii = lax.broadcasted_iota(I32, (LANE, LANE), 0)
    jj = lax.broadcasted_iota(I32, (LANE, LANE), 1)
    return jnp.where(ii == jj, 1.0, 0.0)


def _lane_sums(x):
    ones = jnp.ones((LANE, LANE), BF16)
    hi, lo = _split(x)
    return _mm(hi, ones) + _mm(lo, ones)


def _peer_score_body(off_ref, sh_ref, h_ref, gate_ref, tab_hbm, w_ref, tab, r_sc, sem):
    _load_table_once(tab_hbm, tab, sem)
    sub = lax.broadcasted_iota(I32, (SUBLANE, LANE), 0)

    def token(t, carry):
        x8 = h_ref[t]
        base = t * PEER_SEL
        for g in range(PEER_SEL // SUBLANE):
            prods = []
            for j in _BUTTERFLY_ORDER:
                r = base + g * SUBLANE + j
                bits = (_table_tile(tab, off_ref[0, 0, r]) << sh_ref[0, 0, r]) & HIGH_HALF
                prods.append(_as_f32(bits) * x8)
            r_sc[t, g * SUBLANE:(g + 1) * SUBLANE, :] = _sublane_sums(prods, sub)
        return carry

    lax.fori_loop(0, PEER_TILE, token, 0)
    lane_sums = _lane_sums(r_sc[...].reshape(PEER_TILE * PEER_SEL, LANE))
    a = jnp.sum(lane_sums.reshape(PEER_TILE, PEER_SEL, LANE) * _eye()[None], axis=1)
    w_ref[...] = gate_ref[...] * _gelu(a)


def _peer_mix_body(off_ref, w_ref, low_ref, x_ref, g2_ref, tab_hbm, o_ref, tab, wsp_hi, wsp_lo, sem):
    _load_table_once(tab_hbm, tab, sem)
    eye = _eye()
    n_acc = 4
    w = w_ref[...]
    low = low_ref[...] > 0.5

    def splat(wt):
        diag = (wt[:, None, :] * eye[None]).reshape(PEER_TILE * PEER_SEL, LANE)
        return _lane_sums(diag).reshape(PEER_TILE, PEER_SEL, LANE)

    wsp_hi[...] = splat(jnp.where(low, 0.0, w))
    wsp_lo[...] = splat(jnp.where(low, w, 0.0))

    def token(t, carry):
        base = t * PEER_SEL
        accs = [jnp.zeros((SUBLANE, LANE), F32) for _ in range(n_acc)]
        for k in range(PEER_SEL):
            tile = _table_tile(tab, off_ref[0, 0, base + k])
            term = wsp_hi[t, k:k + 1, :] * _as_f32(tile & HIGH_HALF) + wsp_lo[t, k:k + 1, :] * _as_f32(tile << 16)
            accs[k % n_acc] = accs[k % n_acc] + term
        y = (accs[0] + accs[1]) + (accs[2] + accs[3])
        o_ref[t] = x_ref[t] + g2_ref[...] * y
        return carry

    lax.fori_loop(0, PEER_TILE, token, 0)


def _peer_expert(idx, gate, h, x, g2, u_tab, v_tab, n_lat):
    b, s, d = x.shape
    n_tok = b * s
    tiles = n_tok // PEER_TILE
    tiles_per_batch = s // PEER_TILE
    lat_tiles = n_lat // PEER_TILE
    half = u_tab.shape[0] // 2
    n = PEER_TILE * PEER_SEL
    in_low = idx // half
    off = ((idx % half) * SUBLANE).reshape(tiles, 1, n)
    sh = (in_low * 16).reshape(tiles, 1, n)
    smem = pl.BlockSpec((1, 1, n), lambda i: (i, 0, 0), memory_space=pltpu.SMEM)
    row = pl.BlockSpec((PEER_TILE, PEER_SEL), lambda i: (i, 0))
    vec = pl.BlockSpec((PEER_TILE, SUBLANE, LANE), lambda i: (i, 0, 0))
    table = pl.BlockSpec(memory_space=pl.ANY)
    tab_scratch = pltpu.VMEM((half * SUBLANE, LANE), I32)
    sel_scratch = pltpu.VMEM((PEER_TILE, PEER_SEL, LANE), F32)
    params = pltpu.CompilerParams(dimension_semantics=("arbitrary",), vmem_limit_bytes=PEER_VMEM_LIMIT)

    w = pl.pallas_call(
        _peer_score_body, grid=(tiles,),
        in_specs=[smem, smem, vec, row, table],
        out_specs=row,
        out_shape=jax.ShapeDtypeStruct((n_tok, PEER_SEL), F32),
        scratch_shapes=[tab_scratch, sel_scratch, pltpu.SemaphoreType.DMA(())],
        compiler_params=params, name="peer_score",
    )(off, sh, h.reshape(n_tok, SUBLANE, LANE), gate.reshape(n_tok, PEER_SEL), _pack_table(u_tab))

    g2_spec = pl.BlockSpec(
        (None, None, SUBLANE, LANE),
        lambda i: (i // tiles_per_batch, jnp.minimum((i % tiles_per_batch) // lat_tiles, 1), 0, 0))
    out = pl.pallas_call(
        _peer_mix_body, grid=(tiles,),
        in_specs=[smem, row, row, vec, g2_spec, table],
        out_specs=vec,
        out_shape=jax.ShapeDtypeStruct((n_tok, SUBLANE, LANE), F32),
        scratch_shapes=[tab_scratch, sel_scratch, sel_scratch, pltpu.SemaphoreType.DMA(())],
        compiler_params=params, name="peer_mix",
    )(off, w, in_low.astype(F32).reshape(n_tok, PEER_SEL), x.reshape(n_tok, SUBLANE, LANE),
      g2.reshape(b, 2, SUBLANE, LANE), _pack_table(v_tab))
    return out.reshape(b, s, d)


def _final_body(x_ref, g_ref, o_ref):
    o_ref[...] = _rms(x_ref[...]) * g_ref[...]


def _final_norm(x, g, n_lat):
    b, s, d = x.shape
    return pl.pallas_call(
        _final_body, grid=(b, n_lat // TOK_TILE),
        in_specs=[_tok_spec(d), _full_spec((1, d))],
        out_specs=_tok_spec(d),
        out_shape=jax.ShapeDtypeStruct((b, n_lat, d), F32),
        compiler_params=_params("parallel", "parallel"), name="final_norm",
    )(x, g)


def _gate_row(vals):
    row = jnp.zeros((2, 2, GDN_HEADS), F32).at[:, 0, :].set(vals.astype(F32)).reshape(1, 4 * GDN_HEADS)
    return jnp.pad(row, ((0, 0), (0, LANE - 4 * GDN_HEADS)))


def kernel(x, c, ctx, c_ctx, ada_w, ada_b, norm1_g, norm2_g, final_g, even_w_in, even_conv_w, gdn_a_log,
           gdn_dt_bias, gdn_norm_g, na_rpb, even_w_out, mla_w_in, mla_q_g, mla_kv_g, mla_w_uq, mla_w_ukv,
           mla_w_out, peer_w_q, peer_sub_keys, peer_u, peer_v):
    bsz, n_lat, d = x.shape
    n_ctx = ctx.shape[1]
    depth = ada_w.shape[0]
    assert n_lat % TOK_TILE == 0 and n_ctx % TOK_TILE == 0 and n_lat // GRID_W >= NA_WIN_ROWS
    assert bsz + 1 <= 2 * HALO
    n_lat_tiles = n_lat // TOK_TILE

    cc = jnp.zeros((2 * HALO, d), F32).at[:bsz].set(c).at[bsz].set(c_ctx)
    mod = _ada(cc, ada_w, ada_b)

    def mods(layer, j):
        lat = mod[layer, :bsz, j * d:(j + 1) * d]
        cx = jnp.broadcast_to(mod[layer, bsz, j * d:(j + 1) * d], (bsz, d))
        return jnp.stack([lat, cx], axis=1)[:, :, None, :]

    xs = jnp.concatenate([x, ctx], axis=1)
    rope = _rope_tables(n_lat, n_ctx)

    for layer in range(depth):
        i = layer // 2
        sh1, sc1, g1, sh2, sc2, g2 = (mods(layer, j) for j in range(6))
        n1 = norm1_g[layer][None, :]
        if layer % 2 == 0:
            w_in = even_w_in[i]
            w_pad = jnp.pad(w_in, ((0, 0), (0, LANE - 4 * GDN_HEADS))).astype(BF16)
            p_na, p_qkv, p_z, p_gate = _even_in(xs, n1, sh1, sc1, w_pad, n_lat_tiles)
            y_a = _na_attention(p_na, _na_bias(na_rpb[i]), n_lat, n_ctx)
            gq, gk, gv, gg = _gdn_prep(p_qkv, p_gate, even_conv_w[i], _gate_row(gdn_a_log[i]),
                                       _gate_row(gdn_dt_bias[i]), n_lat_tiles)
            o_f, o_b = _gdn_scan(gq, gk, gv, gg, n_lat, n_ctx)
            xs = _even_out(y_a, o_f, o_b, p_z, gdn_norm_g[i][None, :], even_w_out[i].astype(BF16), xs, g1,
                           n_lat_tiles)
        else:
            weights = _mla_weights(mla_w_in[i], mla_w_uq[i], mla_w_ukv[i])
            q, k, v = _mla_proj(xs, n1, sh1, sc1, weights, mla_q_g[i][None, :], mla_kv_g[i][None, :], rope,
                                n_lat_tiles)
            att = _mla_attention(q, k, v, n_ctx)
            xs = _odd_out(att, mla_w_out[i].astype(BF16), xs, g1, n_lat_tiles)

        keys = peer_sub_keys[layer].reshape(PEER_HEADS * 2, PEER_KEYS, -1).astype(BF16)
        h2, idx_t, gate_t = _peer_route(xs, norm2_g[layer][None, :], sh2, sc2, peer_w_q[layer].astype(BF16), keys,
                                        n_lat_tiles)
        s = n_lat + n_ctx
        idx = idx_t.transpose(0, 1, 3, 2).reshape(bsz, s, PEER_SEL)
        gate = gate_t.transpose(0, 1, 3, 2).reshape(bsz, s, PEER_SEL)
        xs = _peer_expert(idx, gate, h2, xs, g2, peer_u[layer], peer_v[layer], n_lat)

    return _final_norm(xs, final_g[None, :], n_lat)
```

```python
import functools

import jax
import jax.numpy as jnp
from jax import lax
from jax.experimental import pallas as pl
from jax.experimental.pallas import tpu as pltpu

F32 = jnp.float32
BF16 = jnp.bfloat16
I32 = jnp.int32
HIGHEST = lax.Precision.HIGHEST

EPS = 1e-6
GRID_W = 64
NA_HEAD_DIM = 64
NA_HEADS = 8
NA_WIN_ROWS = 8
NA_WIN_COLS = 16
NA_W = NA_HEADS * NA_HEAD_DIM
GDN_HEAD_DIM = 128
GDN_HEADS = 4
GDN_W = GDN_HEADS * GDN_HEAD_DIM
GDN_CONV = 5
GDN_CHUNK = 64
MLA_HEADS = 8
MLA_NOPE = 128
MLA_ROPE = 64
MLA_V = 128
MLA_Q_RANK = 384
MLA_KV_RANK = 256
MLA_QK_PAD = 256
ROPE_THETA = 10000.0
PEER_KEYS = 128
PEER_HEADS = 8
PEER_TOPK = 16
PEER_SEL = PEER_HEADS * PEER_TOPK

TOK_TILE = 256
LANE = 128
SUBLANE = 8
HALO = SUBLANE
PEER_TILE = 64
PEER_VMEM_LIMIT = 56 * 1024 * 1024
NEG = -1e30
FLASH_GROUPS = (3, 2, 1)
VMEM_LIMIT = 56 * 1024 * 1024


def _params(*sem):
    return pltpu.CompilerParams(dimension_semantics=sem, vmem_limit_bytes=VMEM_LIMIT)


def _nt(a, b, precision=None):
    return lax.dot_general(a, b, (((1,), (1,)), ((), ())), precision=precision,
                           preferred_element_type=F32)


def _mm(a, b, precision=None):
    return jnp.dot(a, b, precision=precision, preferred_element_type=F32)


def _rms(x):
    return x * lax.rsqrt(jnp.mean(x * x, axis=-1, keepdims=True) + EPS)


def _norm_mod(x, g, sh, sc):
    return (_rms(x) * g) * (1.0 + sc) + sh


def _silu(x):
    return x * jax.nn.sigmoid(x)


def _ada_body(c_ref, w_ref, b_ref, o_ref):
    o_ref[0] = _mm(_silu(c_ref[...]), w_ref[0], HIGHEST) + b_ref[0]


def _ada(cc, ada_w, ada_b):
    depth, d, n = ada_w.shape
    tn = n // 4
    return pl.pallas_call(
        _ada_body, grid=(depth, n // tn),
        in_specs=[pl.BlockSpec(cc.shape, lambda l, j: (0, 0)),
                  pl.BlockSpec((1, d, tn), lambda l, j: (l, 0, j)),
                  pl.BlockSpec((1, 1, tn), lambda l, j: (l, 0, j))],
        out_specs=pl.BlockSpec((1, cc.shape[0], tn), lambda l, j: (l, 0, j)),
        out_shape=jax.ShapeDtypeStruct((depth, cc.shape[0], n), F32),
        compiler_params=_params("arbitrary", "arbitrary"), name="ada_mod",
    )(cc, ada_w, ada_b.reshape(depth, 1, n))


def _tok_spec(width, col=0):
    return pl.BlockSpec((None, TOK_TILE, width), lambda b, t: (b, t, col))


def _mod_spec(d, n_lat_tiles):
    return pl.BlockSpec((None, None, 1, d), lambda b, t: (b, jnp.minimum(t // n_lat_tiles, 1), 0, 0))


def _full_spec(shape):
    nd = len(shape)
    return pl.BlockSpec(shape, lambda b, t: (0,) * nd)


def _even_in_body(x_ref, g_ref, sh_ref, sc_ref, w_ref, na_ref, qkv_ref, z_ref, gate_ref):
    h = _norm_mod(x_ref[...], g_ref[...], sh_ref[...], sc_ref[...]).astype(BF16)
    c0, c1, c2 = 3 * NA_W, 3 * NA_W + 3 * GDN_W, 3 * NA_W + 4 * GDN_W
    na_ref[...] = _mm(h, w_ref[:, :c0]).astype(BF16)
    qkv_ref[...] = _mm(h, w_ref[:, c0:c1])
    z_ref[...] = _mm(h, w_ref[:, c1:c2])
    gate_ref[...] = _mm(h, w_ref[:, c2:])


def _even_in(x, g, sh, sc, w, n_lat_tiles):
    b, s, d = x.shape
    nt = s // TOK_TILE
    return pl.pallas_call(
        _even_in_body, grid=(b, nt),
        in_specs=[_tok_spec(d), _full_spec((1, d)), _mod_spec(d, n_lat_tiles), _mod_spec(d, n_lat_tiles),
                  _full_spec(w.shape)],
        out_specs=[_tok_spec(3 * NA_W), _tok_spec(3 * GDN_W), _tok_spec(GDN_W), _tok_spec(LANE)],
        out_shape=[jax.ShapeDtypeStruct((b, s, 3 * NA_W), BF16),
                   jax.ShapeDtypeStruct((b, s, 3 * GDN_W), F32),
                   jax.ShapeDtypeStruct((b, s, GDN_W), F32),
                   jax.ShapeDtypeStruct((b, s, LANE), F32)],
        compiler_params=_params("parallel", "parallel"), name="even_in_proj",
    )(x, g, sh, sc, w)


def _softmax_pv(parts):
    m = None
    for s, _ in parts:
        mi = jnp.max(s, axis=-1, keepdims=True)
        m = mi if m is None else jnp.maximum(m, mi)
    l = None
    o = None
    for s, v in parts:
        p = jnp.exp(s - m)
        li = jnp.sum(p, axis=-1, keepdims=True)
        oi = _mm(p.astype(BF16), v)
        l = li if l is None else l + li
        o = oi if o is None else o + oi
    return o / l


def _na_body(q_ref, k_ref, v_ref, bias_ref, o_ref, *, n_lat_tiles, rows, n_lat, n_ctx):
    i = pl.program_id(2)
    lane = lax.broadcasted_iota(I32, (1, LANE), 1)
    lo = lane < NA_HEAD_DIM
    scale = jnp.asarray(NA_HEAD_DIM ** -0.5, BF16)
    kc = k_ref[n_lat:n_lat + n_ctx, :]
    vc = v_ref[n_lat:n_lat + n_ctx, :]
    zero = jnp.zeros((), BF16)
    rows_per_tile = TOK_TILE // GRID_W

    @pl.when(i < n_lat_tiles)
    def _():
        chains = []
        for rr in range(rows_per_tile):
            r = i * rows_per_tile + rr
            r_start = jnp.clip(r - NA_WIN_ROWS // 2, 0, rows - NA_WIN_ROWS)
            delta = r - r_start
            start = pl.multiple_of(r_start * GRID_W, GRID_W)
            kwin = k_ref[pl.ds(start, NA_WIN_ROWS * GRID_W), :]
            vwin = v_ref[pl.ds(start, NA_WIN_ROWS * GRID_W), :]
            q = q_ref[rr * GRID_W:(rr + 1) * GRID_W, :] * scale
            for e, sel in enumerate((lo, jnp.logical_not(lo))):
                chains.append((jnp.where(sel, q, zero), kwin, vwin, bias_ref[delta, e]))
        s_lat = [_nt(qm, kwin) + bias for qm, kwin, _, bias in chains]
        s_ctx = [_nt(qm, kc) for qm, _, _, _ in chains]
        m = [jnp.maximum(jnp.max(a, axis=-1, keepdims=True), jnp.max(c, axis=-1, keepdims=True))
             for a, c in zip(s_lat, s_ctx)]
        p_lat = [jnp.exp(a - mm) for a, mm in zip(s_lat, m)]
        p_ctx = [jnp.exp(c - mm) for c, mm in zip(s_ctx, m)]
        l = [jnp.sum(a, axis=-1, keepdims=True) + jnp.sum(c, axis=-1, keepdims=True) for a, c in zip(p_lat, p_ctx)]
        o = [(_mm(c.astype(BF16), vc) + _mm(a.astype(BF16), ch[2])) / ll
             for a, c, ch, ll in zip(p_lat, p_ctx, chains, l)]
        for rr in range(rows_per_tile):
            o_ref[rr * GRID_W:(rr + 1) * GRID_W, :] = jnp.where(lo, o[2 * rr], o[2 * rr + 1])

    @pl.when(i >= n_lat_tiles)
    def _():
        q = q_ref[...] * scale
        outs = []
        for sel in (lo, jnp.logical_not(lo)):
            qm = jnp.where(sel, q, zero)
            outs.append(_softmax_pv([(_nt(qm, kc), vc)]))
        o_ref[...] = jnp.where(lo, outs[0], outs[1])


def _na_bias(rpb):
    wr, wc = NA_WIN_ROWS, NA_WIN_COLS
    q_col = jnp.arange(GRID_W)
    c_start = jnp.clip(q_col - wc // 2, 0, GRID_W - wc)
    k_col = jnp.tile(jnp.arange(GRID_W), wr)
    k_row = jnp.repeat(jnp.arange(wr), GRID_W)
    col_mask = (k_col[None, :] >= c_start[:, None]) & (k_col[None, :] < c_start[:, None] + wc)
    dc_idx = jnp.clip(k_col[None, :] - q_col[:, None], 1 - wc, wc - 1) + (wc - 1)
    delta = jnp.arange(wr)
    dr_idx = k_row[None, :] - delta[:, None] + (wr - 1)
    oh_r = jax.nn.one_hot(dr_idx, 2 * wr - 1, dtype=F32)
    oh_c = jax.nn.one_hot(dc_idx, 2 * wc - 1, dtype=F32)
    bias = jnp.einsum('dkr,hrc,qkc->hdqk', oh_r, rpb.astype(F32), oh_c, precision=HIGHEST)
    bias = jnp.where(col_mask[None, None], bias, NEG)
    return bias.transpose(1, 0, 2, 3)


def _na_attention(p_na, bias, n_lat, n_ctx):
    b, s, _ = p_na.shape
    nt = s // TOK_TILE
    pairs = NA_W // LANE
    rows = n_lat // GRID_W
    body = functools.partial(_na_body, n_lat_tiles=n_lat // TOK_TILE, rows=rows, n_lat=n_lat, n_ctx=n_ctx)
    return pl.pallas_call(
        body, grid=(b, pairs, nt),
        in_specs=[pl.BlockSpec((None, TOK_TILE, LANE), lambda bb, hp, i: (bb, i, hp)),
                  pl.BlockSpec((None, s, LANE), lambda bb, hp, i: (bb, 0, pairs + hp)),
                  pl.BlockSpec((None, s, LANE), lambda bb, hp, i: (bb, 0, 2 * pairs + hp)),
                  pl.BlockSpec((NA_WIN_ROWS, 2, GRID_W, NA_WIN_ROWS * GRID_W), lambda bb, hp, i: (0, hp, 0, 0))],
        out_specs=pl.BlockSpec((None, TOK_TILE, LANE), lambda bb, hp, i: (bb, i, hp)),
        out_shape=jax.ShapeDtypeStruct((b, s, NA_W), F32),
        compiler_params=_params("parallel", "parallel", "arbitrary"), name="na_attention",
    )(p_na, p_na, p_na, bias)


def _gdn_prep_body(prev_ref, cur_ref, next_ref, gate_ref, cw_ref, alog_ref, dtb_ref,
                   q_ref, k_ref, v_ref, g_ref, *, n_lat_tiles, nt):
    i = pl.program_id(1)
    top = jnp.where(jnp.logical_or(i == 0, i == n_lat_tiles), 0.0, 1.0)
    bot = jnp.where(jnp.logical_or(i == n_lat_tiles - 1, i == nt - 1), 0.0, 1.0)
    ext = jnp.concatenate([prev_ref[...] * top, cur_ref[...], next_ref[...] * bot], axis=0)
    n = TOK_TILE + 2 * HALO
    y = None
    for j in range(GDN_CONV):
        off = j - GDN_CONV // 2
        shifted = ext if off == 0 else pltpu.roll(ext, (-off) % n, axis=0)
        term = cw_ref[j:j + 1, :] * shifted[HALO:HALO + TOK_TILE, :]
        y = term if y is None else y + term
    y = _silu(y)

    def l2(t):
        return t * lax.rsqrt(jnp.sum(t * t, axis=-1, keepdims=True) + EPS)

    for h in range(GDN_HEADS):
        a, bnd = h * GDN_HEAD_DIM, (h + 1) * GDN_HEAD_DIM
        q_ref[:, a:bnd] = l2(y[:, a:bnd]) * (GDN_HEAD_DIM ** -0.5)
        k_ref[:, a:bnd] = l2(y[:, GDN_W + a:GDN_W + bnd])
    v_ref[...] = y[:, 2 * GDN_W:]

    gr = gate_ref[...]
    lane = lax.broadcasted_iota(I32, (1, LANE), 1)
    is_decay = (lane % (2 * GDN_HEADS)) < GDN_HEADS
    t = gr + dtb_ref[...]
    softplus = jnp.maximum(t, 0.0) + jnp.log1p(jnp.exp(-jnp.abs(t)))
    g_ref[...] = jnp.where(is_decay, -jnp.exp(alog_ref[...]) * softplus, jax.nn.sigmoid(gr))


def _gdn_prep(p_qkv, p_gate, conv_w, alog_row, dtb_row, n_lat_tiles):
    b, s, c = p_qkv.shape
    nt = s // TOK_TILE
    per = TOK_TILE // HALO
    body = functools.partial(_gdn_prep_body, n_lat_tiles=n_lat_tiles, nt=nt)
    return pl.pallas_call(
        body, grid=(b, nt),
        in_specs=[pl.BlockSpec((None, HALO, c), lambda bb, t: (bb, jnp.maximum(t * per - 1, 0), 0)),
                  _tok_spec(c),
                  pl.BlockSpec((None, HALO, c), lambda bb, t: (bb, jnp.minimum((t + 1) * per, s // HALO - 1), 0)),
                  _tok_spec(LANE), _full_spec(conv_w.shape), _full_spec((1, LANE)), _full_spec((1, LANE))],
        out_specs=[_tok_spec(GDN_W), _tok_spec(GDN_W), _tok_spec(GDN_W), _tok_spec(LANE)],
        out_shape=[jax.ShapeDtypeStruct((b, s, GDN_W), F32)] * 3 + [jax.ShapeDtypeStruct((b, s, LANE), F32)],
        compiler_params=_params("parallel", "parallel"), name="gdn_prep",
    )(p_qkv, p_qkv, p_qkv, p_gate, conv_w, alog_row, dtb_row)


def _split(x):
    hi = x.astype(BF16)
    return hi, (x - hi.astype(F32)).astype(BF16)


def _mm3(a, b):
    ah, al = _split(a)
    bh, bl = _split(b)
    return _mm(ah, bh) + (_mm(ah, bl) + _mm(al, bh))


def _gdn_chunk_step(chains, states):
    n = len(chains)
    kb = [c[1].astype(BF16) for c in chains]
    dec = [jnp.exp(jnp.where(c[7], c[3] - c[4], NEG)) for c in chains]
    kk = [_nt(x, x) for x in kb]
    npow = [-jnp.where(c[8], c[5] * kk[i] * dec[i], 0.0) for i, c in enumerate(chains)]
    inv = [jnp.where(jnp.logical_and(c[7], jnp.logical_not(c[8])), 1.0, 0.0) + npow[i] for i, c in enumerate(chains)]
    for _ in range(5):
        npow = [_mm3(x, x) for x in npow]
        inv = [inv[i] + _mm3(inv[i], npow[i]) for i in range(n)]
    egc = [jnp.exp(c[3]) for c in chains]
    rhs = [jnp.concatenate([c[5] * c[2], (c[5] * egc[i]) * c[1]], axis=1) for i, c in enumerate(chains)]
    w = [_mm3(inv[i], rhs[i]) for i in range(n)]
    sb = [x.astype(BF16) for x in states]
    u = [w[i][:, :GDN_HEAD_DIM] - _mm(w[i][:, GDN_HEAD_DIM:].astype(BF16), sb[i]) for i in range(n)]
    ub = [x.astype(BF16) for x in u]
    p_in = [(_nt(c[0].astype(BF16), kb[i]) * dec[i]).astype(BF16) for i, c in enumerate(chains)]
    o = [_mm((c[0] * egc[i]).astype(BF16), sb[i]) + _mm(p_in[i], ub[i]) for i, c in enumerate(chains)]
    k_end_t = [(c[1] * jnp.exp(c[6] - c[3])).T.astype(BF16) for c in chains]
    new_states = [jnp.exp(c[6]) * states[i] + _mm(k_end_t[i], ub[i]) for i, c in enumerate(chains)]
    return o, new_states


def _gdn_scan_body(qf, kf, vf, gf, qb, kb, vb, gb, of, ob, s_sc):
    @pl.when(pl.program_id(1) == 0)
    def _():
        s_sc[...] = jnp.zeros_like(s_sc)

    ii = lax.broadcasted_iota(I32, (GDN_CHUNK, GDN_CHUNK), 0)
    jj = lax.broadcasted_iota(I32, (GDN_CHUNK, GDN_CHUNK), 1)
    chains = []
    for d, (q_ref, k_ref, v_ref, g_ref) in enumerate(((qf, kf, vf, gf), (qb, kb, vb, gb))):
        incl = (ii >= jj) if d == 0 else (ii <= jj)
        strict = (ii > jj) if d == 0 else (ii < jj)
        gates = g_ref[...]
        gcum = _mm(jnp.where(incl, 1.0, 0.0), gates, HIGHEST)
        gcum_t = gcum.T
        gtot = jnp.sum(gates, axis=0, keepdims=True)
        for h in range(GDN_HEADS):
            ca = d * 2 * GDN_HEADS + h
            cb = ca + GDN_HEADS
            sl = slice(h * GDN_HEAD_DIM, (h + 1) * GDN_HEAD_DIM)
            chains.append((q_ref[:, sl], k_ref[:, sl], v_ref[:, sl], gcum[:, ca:ca + 1], gcum_t[ca:ca + 1, :],
                           gates[:, cb:cb + 1], gtot[:, ca:ca + 1], incl, strict))
    outs, new_states = _gdn_chunk_step(chains, [s_sc[i] for i in range(2 * GDN_HEADS)])
    for i in range(2 * GDN_HEADS):
        s_sc[i] = new_states[i]
        o_ref = of if i < GDN_HEADS else ob
        h = i % GDN_HEADS
        o_ref[:, h * GDN_HEAD_DIM:(h + 1) * GDN_HEAD_DIM] = outs[i]


def _gdn_scan(q, k, v, g, n_lat, n_ctx):
    b, s, w = q.shape
    nc = s // GDN_CHUNK
    nlc, ncc = n_lat // GDN_CHUNK, n_ctx // GDN_CHUNK

    def fwd(j):
        return jnp.where(j < ncc, nlc + j, j - ncc)

    def bwd(j):
        return nc - 1 - j

    def spec(width, order):
        return pl.BlockSpec((None, GDN_CHUNK, width), lambda bb, j: (bb, order(j), 0))

    return pl.pallas_call(
        _gdn_scan_body, grid=(b, nc),
        in_specs=[spec(w, fwd), spec(w, fwd), spec(w, fwd), spec(LANE, fwd),
                  spec(w, bwd), spec(w, bwd), spec(w, bwd), spec(LANE, bwd)],
        out_specs=[spec(w, fwd), spec(w, bwd)],
        out_shape=[jax.ShapeDtypeStruct((b, s, w), F32)] * 2,
        scratch_shapes=[pltpu.VMEM((2 * GDN_HEADS, GDN_HEAD_DIM, GDN_HEAD_DIM), F32)],
        compiler_params=_params("parallel", "arbitrary"), name="gdn_scan",
    )(q, k, v, g, q, k, v, g)


def _even_out_body(ya_ref, of_ref, ob_ref, z_ref, ng_ref, w_ref, x_ref, g1_ref, o_ref):
    o = of_ref[...] + ob_ref[...]
    gate = _silu(z_ref[...])
    parts = []
    for h in range(GDN_HEADS):
        sl = slice(h * GDN_HEAD_DIM, (h + 1) * GDN_HEAD_DIM)
        parts.append(_rms(o[:, sl]) * ng_ref[...] * gate[:, sl])
    yb = jnp.concatenate(parts, axis=1).astype(BF16)
    y = _mm(ya_ref[...].astype(BF16), w_ref[:NA_W, :]) + _mm(yb, w_ref[NA_W:, :])
    o_ref[...] = x_ref[...] + g1_ref[...] * y


def _even_out(ya, o_f, o_b, z, norm_g, w_out, x, g1, n_lat_tiles):
    b, s, d = x.shape
    return pl.pallas_call(
        _even_out_body, grid=(b, s // TOK_TILE),
        in_specs=[_tok_spec(NA_W), _tok_spec(GDN_W), _tok_spec(GDN_W), _tok_spec(GDN_W),
                  _full_spec((1, GDN_HEAD_DIM)), _full_spec(w_out.shape), _tok_spec(d), _mod_spec(d, n_lat_tiles)],
        out_specs=_tok_spec(d),
        out_shape=jax.ShapeDtypeStruct((b, s, d), F32),
        compiler_params=_params("parallel", "parallel"), name="even_out_proj",
    )(ya, o_f, o_b, z, norm_g, w_out, x, g1)


def _rot_cols(w):
    a, b, c, d = jnp.split(w, 4, axis=-1)
    return jnp.concatenate([-b, a, -d, c], axis=-1)


def _mla_proj_body(x_ref, g_ref, sh_ref, sc_ref, win_ref, qg_ref, kvg_ref, w1_ref, w2_ref, wkv_ref,
                   cq_ref, sq_ref, ck_ref, sk_ref, q_out, k_out, vt_out):
    h = _norm_mod(x_ref[...], g_ref[...], sh_ref[...], sc_ref[...]).astype(BF16)
    p = _mm(h, win_ref[...])
    cq = (_rms(p[:, :MLA_Q_RANK]) * qg_ref[...]).astype(BF16)
    q1 = _mm(cq, w1_ref[...])
    q2 = _mm(cq, w2_ref[...])
    cos_q = jnp.concatenate([cq_ref[...]] * MLA_HEADS, axis=1)
    sin_q = jnp.concatenate([sq_ref[...]] * MLA_HEADS, axis=1)
    q_out[...] = (q1 * cos_q + q2 * sin_q).astype(BF16)
    kv_lo = MLA_Q_RANK + MLA_KV_RANK
    ckv = (_rms(p[:, MLA_Q_RANK:kv_lo]) * kvg_ref[...]).astype(BF16)
    kv = _mm(ckv, wkv_ref[...])
    kr = (p[:, kv_lo:kv_lo + LANE] * ck_ref[...] + p[:, kv_lo + LANE:] * sk_ref[...]).astype(BF16)
    per = MLA_NOPE + MLA_V
    for hh in range(MLA_HEADS):
        k_out[:, hh * MLA_QK_PAD:hh * MLA_QK_PAD + MLA_NOPE] = kv[:, hh * per:hh * per + MLA_NOPE].astype(BF16)
        k_out[:, hh * MLA_QK_PAD + MLA_NOPE:(hh + 1) * MLA_QK_PAD] = kr
        vt_out[hh * MLA_V:(hh + 1) * MLA_V, :] = kv[:, hh * per + MLA_NOPE:(hh + 1) * per].T.astype(BF16)


def _mla_weights(w_in, w_uq, w_ukv):
    d = w_in.shape[0]
    kv_lo = MLA_Q_RANK + MLA_KV_RANK
    kr = w_in[:, kv_lo:]
    zpad = jnp.zeros((d, LANE - MLA_ROPE), F32)
    win_ext = jnp.concatenate([w_in[:, :kv_lo], kr, zpad, _rot_cols(kr), zpad], axis=1).astype(BF16)
    wq = w_uq.reshape(MLA_Q_RANK, MLA_HEADS, MLA_NOPE + MLA_ROPE)
    nope, rope = wq[..., :MLA_NOPE], wq[..., MLA_NOPE:]
    z64 = jnp.zeros((MLA_Q_RANK, MLA_HEADS, MLA_QK_PAD - MLA_NOPE - MLA_ROPE), F32)
    w1 = jnp.concatenate([nope, rope, z64], axis=-1).reshape(MLA_Q_RANK, MLA_HEADS * MLA_QK_PAD).astype(BF16)
    w2 = jnp.concatenate([jnp.zeros_like(nope), _rot_cols(rope), z64], axis=-1)
    w2 = w2.reshape(MLA_Q_RANK, MLA_HEADS * MLA_QK_PAD).astype(BF16)
    return win_ext, w1, w2, w_ukv.astype(BF16)


def _rope_tables(n_lat, n_ctx):
    half = MLA_ROPE // 2
    inv_freq = ROPE_THETA ** (-jnp.arange(0, half, 2, dtype=F32) / half)
    t = jnp.arange(n_lat, dtype=I32)
    row = (t // GRID_W).astype(F32)
    col = (t % GRID_W).astype(F32)
    ang_r = row[:, None] * inv_freq[None, :]
    ang_c = col[:, None] * inv_freq[None, :]
    ang = jnp.concatenate([ang_r, ang_r, ang_c, ang_c], axis=-1)
    cos = jnp.concatenate([jnp.cos(ang), jnp.ones((n_ctx, MLA_ROPE), F32)], axis=0)
    sin = jnp.concatenate([jnp.sin(ang), jnp.zeros((n_ctx, MLA_ROPE), F32)], axis=0)
    s = n_lat + n_ctx
    scale = (MLA_NOPE + MLA_ROPE) ** -0.5
    pad_q = jnp.zeros((s, MLA_QK_PAD - MLA_NOPE - MLA_ROPE), F32)
    cos_q = jnp.concatenate([jnp.ones((s, MLA_NOPE), F32), cos, pad_q], axis=1) * scale
    sin_q = jnp.concatenate([jnp.zeros((s, MLA_NOPE), F32), sin, pad_q], axis=1) * scale
    pad_k = jnp.zeros((s, LANE - MLA_ROPE), F32)
    cos_k = jnp.concatenate([cos, pad_k], axis=1)
    sin_k = jnp.concatenate([sin, pad_k], axis=1)
    return cos_q, sin_q, cos_k, sin_k


def _mla_proj(x, g, sh, sc, weights, q_g, kv_g, tables, n_lat_tiles):
    b, s, d = x.shape
    win_ext, w1, w2, wkv = weights
    cos_q, sin_q, cos_k, sin_k = tables
    tab = lambda width: pl.BlockSpec((TOK_TILE, width), lambda bb, t: (t, 0))
    qw = MLA_HEADS * MLA_QK_PAD
    return pl.pallas_call(
        _mla_proj_body, grid=(b, s // TOK_TILE),
        in_specs=[_tok_spec(d), _full_spec((1, d)), _mod_spec(d, n_lat_tiles), _mod_spec(d, n_lat_tiles),
                  _full_spec(win_ext.shape), _full_spec((1, MLA_Q_RANK)), _full_spec((1, MLA_KV_RANK)),
                  _full_spec(w1.shape), _full_spec(w2.shape), _full_spec(wkv.shape),
                  tab(MLA_QK_PAD), tab(MLA_QK_PAD), tab(LANE), tab(LANE)],
        out_specs=[_tok_spec(qw), _tok_spec(qw),
                   pl.BlockSpec((None, None, MLA_HEADS * MLA_V, TOK_TILE), lambda bb, t: (bb, t, 0, 0))],
        out_shape=[jax.ShapeDtypeStruct((b, s, qw), BF16), jax.ShapeDtypeStruct((b, s, qw), BF16),
                   jax.ShapeDtypeStruct((b, s // TOK_TILE, MLA_HEADS * MLA_V, TOK_TILE), BF16)],
        compiler_params=_params("parallel", "parallel"), name="mla_proj",
    )(x, g, sh, sc, win_ext, q_g, kv_g, w1, w2, wkv, cos_q, sin_q, cos_k, sin_k)


def _flash_body(q_ref, k_ref, vt_ref, o_ref, m_sc, l_sc, acc_sc, *, group, nq, n_lat):
    qi = pl.program_id(2)
    c = TOK_TILE
    span = group * c
    q = [q_ref[g * c:(g + 1) * c, :] for g in range(group)]
    m_sc[...] = jnp.full_like(m_sc, NEG)
    l_sc[...] = jnp.zeros_like(l_sc)
    acc_sc[...] = jnp.zeros_like(acc_sc)
    is_last = qi == nq - 1

    def step(it, carry):
        kc = k_ref[pl.ds(pl.multiple_of(it * span, span), span), :]
        st = [_nt(kc, q[g]) for g in range(group)]
        key = lax.broadcasted_iota(I32, (span, c), 0) + it * span
        st[-1] = jnp.where(jnp.logical_and(is_last, key < n_lat), NEG, st[-1])
        m_prev = [m_sc[g, 0:1, :] for g in range(group)]
        m_new = [jnp.maximum(m_prev[g], jnp.max(st[g], axis=0, keepdims=True)) for g in range(group)]
        alpha = [jnp.exp(m_prev[g] - m_new[g]) for g in range(group)]
        p = [jnp.exp(st[g] - m_new[g]) for g in range(group)]
        pt = [x.astype(BF16) for x in p]
        for g in range(group):
            l_sc[g, 0:1, :] = alpha[g] * l_sc[g, 0:1, :] + jnp.sum(p[g], axis=0, keepdims=True)
            pv = None
            for j in range(group):
                term = _mm(vt_ref[it * group + j], pt[g][j * c:(j + 1) * c, :])
                pv = term if pv is None else pv + term
            acc_sc[g] = alpha[g] * acc_sc[g] + pv
            m_sc[g, 0:1, :] = m_new[g]
        return carry

    lax.fori_loop(0, nq, step, 0)
    for g in range(group):
        o_ref[g * c:(g + 1) * c, :] = (acc_sc[g] / l_sc[g, 0:1, :]).T.astype(o_ref.dtype)


def _mla_attention(q, k, vt, n_lat):
    b, s, _ = q.shape
    c = TOK_TILE
    assert s - n_lat == c
    group = next(g for g in FLASH_GROUPS if (s // c) % g == 0)
    t = group * c
    nq = s // t
    body = functools.partial(_flash_body, group=group, nq=nq, n_lat=n_lat)
    return pl.pallas_call(
        body, grid=(b, MLA_HEADS, nq),
        in_specs=[pl.BlockSpec((None, t, MLA_QK_PAD), lambda bb, h, i: (bb, i, h)),
                  pl.BlockSpec((None, s, MLA_QK_PAD), lambda bb, h, i: (bb, 0, h)),
                  pl.BlockSpec((None, s // c, MLA_V, c), lambda bb, h, i: (bb, 0, h, 0))],
        out_specs=pl.BlockSpec((None, t, MLA_V), lambda bb, h, i: (bb, i, h)),
        out_shape=jax.ShapeDtypeStruct((b, s, MLA_HEADS * MLA_V), BF16),
        scratch_shapes=[pltpu.VMEM((group, SUBLANE, c), F32), pltpu.VMEM((group, SUBLANE, c), F32),
                        pltpu.VMEM((group, MLA_V, c), F32)],
        compiler_params=_params("parallel", "parallel", "arbitrary"), name="mla_flash",
    )(q, k, vt)


def _odd_out_body(a_ref, w_ref, x_ref, g1_ref, o_ref):
    o_ref[...] = x_ref[...] + g1_ref[...] * _mm(a_ref[...], w_ref[...])


def _odd_out(a, w_out, x, g1, n_lat_tiles):
    b, s, d = x.shape
    return pl.pallas_call(
        _odd_out_body, grid=(b, s // TOK_TILE),
        in_specs=[_tok_spec(a.shape[-1]), _full_spec(w_out.shape), _tok_spec(d), _mod_spec(d, n_lat_tiles)],
        out_specs=_tok_spec(d),
        out_shape=jax.ShapeDtypeStruct((b, s, d), F32),
        compiler_params=_params("parallel", "parallel"), name="odd_out_proj",
    )(a, w_out, x, g1)


def _top_rows(s, k, index=None):
    if index is None:
        index = lax.broadcasted_iota(I32, s.shape, 0).astype(F32)
    vals, idxs = [], []
    for _ in range(k):
        m = jnp.max(s, axis=0, keepdims=True)
        i = jnp.min(jnp.where(s == m, index, 2.0 ** 20), axis=0, keepdims=True)
        vals.append(m)
        idxs.append(i)
        s = jnp.where(index == i, -jnp.inf, s)
    return jnp.concatenate(vals, axis=0), jnp.concatenate(idxs, axis=0)


def _candidate_blocks(kk):
    return [(0, kk)] + [(a, SUBLANE) for a in range(1, kk)]


def _peer_route_body(x_ref, g_ref, sh_ref, sc_ref, wq_ref, keys_ref, h_ref, idx_ref, gate_ref):
    h = _norm_mod(x_ref[...], g_ref[...], sh_ref[...], sc_ref[...])
    h_ref[...] = h
    qy = _mm(h.astype(BF16), wq_ref[...]).astype(BF16)
    kk = PEER_TOPK
    t = qy.shape[0]
    idx_rows, gate_rows = [], []
    blocks = _candidate_blocks(kk)
    ranks = {nb: lax.broadcasted_iota(I32, (nb, t), 0).astype(F32) for nb in (kk, SUBLANE)}
    cand_flat = jnp.concatenate([ranks[nb] + float(a * kk) for a, nb in blocks], axis=0)
    cand_ok = jnp.concatenate([(ranks[nb] + 1.0) * float(a + 1) for a, nb in blocks], axis=0) <= kk
    for hd in range(PEER_HEADS):
        tops = []
        for part in range(2):
            c = (hd * 2 + part) * PEER_KEYS
            s = _nt(keys_ref[hd * 2 + part], qy[:, c:c + PEER_KEYS])
            tops.append(_top_rows(s, kk))
        (s1, i1), (s2, i2) = tops
        comb = jnp.concatenate([s1[a:a + 1, :] + s2[:nb, :] for a, nb in blocks], axis=0)
        c_top, c_idx = _top_rows(jnp.where(cand_ok, comb, -jnp.inf), kk, cand_flat)
        a_sel = jnp.floor(c_idx * (1.0 / kk))
        b_sel = c_idx - a_sel * kk
        e1 = jnp.zeros((kk, t), F32)
        e2 = jnp.zeros((kk, t), F32)
        for r in range(kk):
            e1 = jnp.where(a_sel == r, i1[r:r + 1, :], e1)
            e2 = jnp.where(b_sel == r, i2[r:r + 1, :], e2)
        idx_rows.append(e1 * PEER_KEYS + e2)
        ex = jnp.exp(c_top - c_top[0:1, :])
        gate_rows.append(ex / jnp.sum(ex, axis=0, keepdims=True))
    idx_ref[...] = jnp.concatenate(idx_rows, axis=0).astype(I32)
    gate_ref[...] = jnp.concatenate(gate_rows, axis=0)


def _peer_route(x, g, sh, sc, w_q, keys, n_lat_tiles):
    b, s, d = x.shape
    nt = s // TOK_TILE
    t_spec = pl.BlockSpec((None, None, PEER_SEL, TOK_TILE), lambda bb, t: (bb, t, 0, 0))
    return pl.pallas_call(
        _peer_route_body, grid=(b, nt),
        in_specs=[_tok_spec(d), _full_spec((1, d)), _mod_spec(d, n_lat_tiles), _mod_spec(d, n_lat_tiles),
                  _full_spec(w_q.shape), _full_spec(keys.shape)],
        out_specs=[_tok_spec(d), t_spec, t_spec],
        out_shape=[jax.ShapeDtypeStruct((b, s, d), F32),
                   jax.ShapeDtypeStruct((b, nt, PEER_SEL, TOK_TILE), I32),
                   jax.ShapeDtypeStruct((b, nt, PEER_SEL, TOK_TILE), F32)],
        compiler_params=_params("parallel", "parallel"), name="peer_route",
    )(x, g, sh, sc, w_q, keys)


def _gelu(a):
    return 0.5 * a * (1.0 + lax.erf(a * (2.0 ** -0.5)))


def _pack_table(tab):
    e, d = tab.shape
    assert d == SUBLANE * LANE and e % 2 == 0
    bits = lax.bitcast_convert_type(tab.astype(BF16), jnp.uint16).astype(jnp.uint32)
    packed = (bits[:e // 2] << 16) | bits[e // 2:]
    return lax.bitcast_convert_type(packed, I32).reshape(e // 2 * SUBLANE, LANE)


HIGH_HALF = -65536


def _table_tile(tab_ref, off):
    return tab_ref[pl.ds(pl.multiple_of(off, SUBLANE), SUBLANE), :]


def _as_f32(bits):
    return lax.bitcast_convert_type(bits, F32)


def _load_table_once(tab_hbm, tab, sem):
    @pl.when(pl.program_id(0) == 0)
    def _():
        cp = pltpu.make_async_copy(tab_hbm, tab, sem)
        cp.start()
        cp.wait()


def _pair_sums(a, b, k, sub):
    take_a = (sub % (2 * k)) < k
    return jnp.where(take_a, a + pltpu.roll(a, SUBLANE - k, axis=0), b + pltpu.roll(b, k, axis=0))


_BUTTERFLY_ORDER = (0, 4, 2, 6, 1, 5, 3, 7)


def _sublane_sums(p, sub):
    l1 = [_pair_sums(p[2 * i], p[2 * i + 1], 4, sub) for i in range(4)]
    l2 = [_pair_sums(l1[0], l1[1], 2, sub), _pair_sums(l1[2], l1[3], 2, sub)]
    return _pair_sums(l2[0], l2[1], 1, sub)


def _eye():
    ii = lax.broadcasted_iota(I32, (LANE, LANE), 0)
    jj = lax.broadcasted_iota(I32, (LANE, LANE), 1)
    return jnp.where(ii == jj, 1.0, 0.0)


def _lane_sums(x):
    ones = jnp.ones((LANE, LANE), BF16)
    hi, lo = _split(x)
    return _mm(hi, ones) + _mm(lo, ones)


def _peer_score_body(off_ref, sh_ref, h_ref, gate_ref, tab_hbm, w_ref, tab, r_sc, sem):
    _load_table_once(tab_hbm, tab, sem)
    sub = lax.broadcasted_iota(I32, (SUBLANE, LANE), 0)

    def token(t, carry):
        x8 = h_ref[t]
        base = t * PEER_SEL
        for g in range(PEER_SEL // SUBLANE):
            prods = []
            for j in _BUTTERFLY_ORDER:
                r = base + g * SUBLANE + j
                bits = (_table_tile(tab, off_ref[0, 0, r]) << sh_ref[0, 0, r]) & HIGH_HALF
                prods.append(_as_f32(bits) * x8)
            r_sc[t, g * SUBLANE:(g + 1) * SUBLANE, :] = _sublane_sums(prods, sub)
        return carry

    lax.fori_loop(0, PEER_TILE, token, 0)
    lane_sums = _lane_sums(r_sc[...].reshape(PEER_TILE * PEER_SEL, LANE))
    a = jnp.sum(lane_sums.reshape(PEER_TILE, PEER_SEL, LANE) * _eye()[None], axis=1)
    w_ref[...] = gate_ref[...] * _gelu(a)


def _peer_mix_body(off_ref, w_ref, low_ref, x_ref, g2_ref, tab_hbm, o_ref, tab, wsp_hi, wsp_lo, sem):
    _load_table_once(tab_hbm, tab, sem)
    eye = _eye()
    n_acc = 4
    w = w_ref[...]
    low = low_ref[...] > 0.5

    def splat(wt):
        diag = (wt[:, None, :] * eye[None]).reshape(PEER_TILE * PEER_SEL, LANE)
        return _lane_sums(diag).reshape(PEER_TILE, PEER_SEL, LANE)

    wsp_hi[...] = splat(jnp.where(low, 0.0, w))
    wsp_lo[...] = splat(jnp.where(low, w, 0.0))

    def token(t, carry):
        base = t * PEER_SEL
        accs = [jnp.zeros((SUBLANE, LANE), F32) for _ in range(n_acc)]
        for k in range(PEER_SEL):
            tile = _table_tile(tab, off_ref[0, 0, base + k])
            term = wsp_hi[t, k:k + 1, :] * _as_f32(tile & HIGH_HALF) + wsp_lo[t, k:k + 1, :] * _as_f32(tile << 16)
            accs[k % n_acc] = accs[k % n_acc] + term
        y = (accs[0] + accs[1]) + (accs[2] + accs[3])
        o_ref[t] = x_ref[t] + g2_ref[...] * y
        return carry

    lax.fori_loop(0, PEER_TILE, token, 0)


def _peer_expert(idx, gate, h, x, g2, u_tab, v_tab, n_lat):
    b, s, d = x.shape
    n_tok = b * s
    tiles = n_tok // PEER_TILE
    tiles_per_batch = s // PEER_TILE
    lat_tiles = n_lat // PEER_TILE
    half = u_tab.shape[0] // 2
    n = PEER_TILE * PEER_SEL
    in_low = idx // half
    off = ((idx % half) * SUBLANE).reshape(tiles, 1, n)
    sh = (in_low * 16).reshape(tiles, 1, n)
    smem = pl.BlockSpec((1, 1, n), lambda i: (i, 0, 0), memory_space=pltpu.SMEM)
    row = pl.BlockSpec((PEER_TILE, PEER_SEL), lambda i: (i, 0))
    vec = pl.BlockSpec((PEER_TILE, SUBLANE, LANE), lambda i: (i, 0, 0))
    table = pl.BlockSpec(memory_space=pl.ANY)
    tab_scratch = pltpu.VMEM((half * SUBLANE, LANE), I32)
    sel_scratch = pltpu.VMEM((PEER_TILE, PEER_SEL, LANE), F32)
    params = pltpu.CompilerParams(dimension_semantics=("arbitrary",), vmem_limit_bytes=PEER_VMEM_LIMIT)

    w = pl.pallas_call(
        _peer_score_body, grid=(tiles,),
        in_specs=[smem, smem, vec, row, table],
        out_specs=row,
        out_shape=jax.ShapeDtypeStruct((n_tok, PEER_SEL), F32),
        scratch_shapes=[tab_scratch, sel_scratch, pltpu.SemaphoreType.DMA(())],
        compiler_params=params, name="peer_score",
    )(off, sh, h.reshape(n_tok, SUBLANE, LANE), gate.reshape(n_tok, PEER_SEL), _pack_table(u_tab))

    g2_spec = pl.BlockSpec(
        (None, None, SUBLANE, LANE),
        lambda i: (i // tiles_per_batch, jnp.minimum((i % tiles_per_batch) // lat_tiles, 1), 0, 0))
    out = pl.pallas_call(
        _peer_mix_body, grid=(tiles,),
        in_specs=[smem, row, row, vec, g2_spec, table],
        out_specs=vec,
        out_shape=jax.ShapeDtypeStruct((n_tok, SUBLANE, LANE), F32),
        scratch_shapes=[tab_scratch, sel_scratch, sel_scratch, pltpu.SemaphoreType.DMA(())],
        compiler_params=params, name="peer_mix",
    )(off, w, in_low.astype(F32).reshape(n_tok, PEER_SEL), x.reshape(n_tok, SUBLANE, LANE),
      g2.reshape(b, 2, SUBLANE, LANE), _pack_table(v_tab))
    return out.reshape(b, s, d)


def _final_body(x_ref, g_ref, o_ref):
    o_ref[...] = _rms(x_ref[...]) * g_ref[...]


def _final_norm(x, g, n_lat):
    b, s, d = x.shape
    return pl.pallas_call(
        _final_body, grid=(b, n_lat // TOK_TILE),
        in_specs=[_tok_spec(d), _full_spec((1, d))],
        out_specs=_tok_spec(d),
        out_shape=jax.ShapeDtypeStruct((b, n_lat, d), F32),
        compiler_params=_params("parallel", "parallel"), name="final_norm",
    )(x, g)


def _gate_row(vals):
    row = jnp.zeros((2, 2, GDN_HEADS), F32).at[:, 0, :].set(vals.astype(F32)).reshape(1, 4 * GDN_HEADS)
    return jnp.pad(row, ((0, 0), (0, LANE - 4 * GDN_HEADS)))


def kernel(x, c, ctx, c_ctx, ada_w, ada_b, norm1_g, norm2_g, final_g, even_w_in, even_conv_w, gdn_a_log,
           gdn_dt_bias, gdn_norm_g, na_rpb, even_w_out, mla_w_in, mla_q_g, mla_kv_g, mla_w_uq, mla_w_ukv,
           mla_w_out, peer_w_q, peer_sub_keys, peer_u, peer_v):
    bsz, n_lat, d = x.shape
    n_ctx = ctx.shape[1]
    depth = ada_w.shape[0]
    assert n_lat % TOK_TILE == 0 and n_ctx % TOK_TILE == 0 and n_lat // GRID_W >= NA_WIN_ROWS
    assert bsz + 1 <= 2 * HALO
    n_lat_tiles = n_lat // TOK_TILE

    cc = jnp.zeros((2 * HALO, d), F32).at[:bsz].set(c).at[bsz].set(c_ctx)
    mod = _ada(cc, ada_w, ada_b)

    def mods(layer, j):
        lat = mod[layer, :bsz, j * d:(j + 1) * d]
        cx = jnp.broadcast_to(mod[layer, bsz, j * d:(j + 1) * d], (bsz, d))
        return jnp.stack([lat, cx], axis=1)[:, :, None, :]

    xs = jnp.concatenate([x, ctx], axis=1)
    rope = _rope_tables(n_lat, n_ctx)

    for layer in range(depth):
        i = layer // 2
        sh1, sc1, g1, sh2, sc2, g2 = (mods(layer, j) for j in range(6))
        n1 = norm1_g[layer][None, :]
        if layer % 2 == 0:
            w_in = even_w_in[i]
            w_pad = jnp.pad(w_in, ((0, 0), (0, LANE - 4 * GDN_HEADS))).astype(BF16)
            p_na, p_qkv, p_z, p_gate = _even_in(xs, n1, sh1, sc1, w_pad, n_lat_tiles)
            y_a = _na_attention(p_na, _na_bias(na_rpb[i]), n_lat, n_ctx)
            gq, gk, gv, gg = _gdn_prep(p_qkv, p_gate, even_conv_w[i], _gate_row(gdn_a_log[i]),
                                       _gate_row(gdn_dt_bias[i]), n_lat_tiles)
            o_f, o_b = _gdn_scan(gq, gk, gv, gg, n_lat, n_ctx)
            xs = _even_out(y_a, o_f, o_b, p_z, gdn_norm_g[i][None, :], even_w_out[i].astype(BF16), xs, g1,
                           n_lat_tiles)
        else:
            weights = _mla_weights(mla_w_in[i], mla_w_uq[i], mla_w_ukv[i])
            q, k, v = _mla_proj(xs, n1, sh1, sc1, weights, mla_q_g[i][None, :], mla_kv_g[i][None, :], rope,
                                n_lat_tiles)
            att = _mla_attention(q, k, v, n_lat)
            xs = _odd_out(att, mla_w_out[i].astype(BF16), xs, g1, n_lat_tiles)

        keys = peer_sub_keys[layer].reshape(PEER_HEADS * 2, PEER_KEYS, -1).astype(BF16)
        h2, idx_t, gate_t = _peer_route(xs, norm2_g[layer][None, :], sh2, sc2, peer_w_q[layer].astype(BF16), keys,
                                        n_lat_tiles)
        s = n_lat + n_ctx
        idx = idx_t.transpose(0, 1, 3, 2).reshape(bsz, s, PEER_SEL)
        gate = gate_t.transpose(0, 1, 3, 2).reshape(bsz, s, PEER_SEL)
        xs = _peer_expert(idx, gate, h2, xs, g2, peer_u[layer], peer_v[layer], n_lat)

    return _final_norm(xs, final_g[None, :], n_lat)
```
